```python
import math
import jax, jax.numpy as jnp
from jax import lax
import numpy as np

D_MODEL = 1024
BATCH = 8
SEQ = 8192
DEPTH = 2

CHUNK = 64
N_MIXERS = 2
EPS = 1e-6

RET_HEADS = 4
RET_QK_DIM = D_MODEL // RET_HEADS
RET_V_DIM = 2 * D_MODEL // RET_HEADS
RET_QK_WIDTH = RET_HEADS * RET_QK_DIM
RET_WIDTH = RET_HEADS * RET_V_DIM
ROPE_BASE = 10000.0

ATT_HEADS = 16
ATT_HEAD_DIM = 2 * D_MODEL // ATT_HEADS
ATT_WIDTH = ATT_HEADS * ATT_HEAD_DIM
LEFT_CHUNKS = 8
BAND = (LEFT_CHUNKS + 1) * CHUNK
MAX_REL = 2 * CHUNK

N_RET_LAYERS = (DEPTH + 1) // 2
N_ATT_LAYERS = DEPTH // 2

kernel_name = "hybrid_retention_chunked_attention_adaln"


def rms_norm_f32(t, gain):
    t32 = t.astype(jnp.float32)
    return t32 * lax.rsqrt(jnp.mean(t32 * t32, axis=-1, keepdims=True) + EPS) * gain.astype(jnp.float32)


def apply_rope(t, cos, sin):
    t1, t2 = jnp.split(t, 2, axis=-1)
    return jnp.concatenate([t1 * cos - t2 * sin, t1 * sin + t2 * cos], axis=-1)


def retention_mixer(h, positions, w_in, gn_g, w_out):
    B, S, _ = h.shape
    nc = S // CHUNK
    proj = h @ w_in
    q, k, v, g = jnp.split(proj, [RET_QK_WIDTH, 2 * RET_QK_WIDTH, 2 * RET_QK_WIDTH + RET_WIDTH], axis=-1)
    q = q.reshape(B, S, RET_HEADS, RET_QK_DIM).astype(jnp.float32)
    k = k.reshape(B, S, RET_HEADS, RET_QK_DIM).astype(jnp.float32)
    v = v.reshape(B, S, RET_HEADS, RET_V_DIM).astype(jnp.float32)

    inv_freq = 1.0 / (ROPE_BASE ** (jnp.arange(0, RET_QK_DIM, 2, dtype=jnp.float32) / RET_QK_DIM))
    ang = positions.astype(jnp.float32)[..., None] * inv_freq
    cos, sin = jnp.cos(ang)[:, :, None, :], jnp.sin(ang)[:, :, None, :]
    q = apply_rope(q, cos, sin)
    k = apply_rope(k, cos, sin) * (RET_QK_DIM ** -0.5)

    def to_chunks(t):
        return t.reshape(B, nc, CHUNK, RET_HEADS, t.shape[-1]).transpose(0, 3, 1, 2, 4)

    qc, kc, vc = to_chunks(q), to_chunks(k), to_chunks(v)

    log_gamma = jnp.log(1.0 - 2.0 ** (-5.0 - jnp.arange(RET_HEADS, dtype=jnp.float32)))
    idx = jnp.arange(CHUNK, dtype=jnp.float32)
    intra_decay = jnp.exp(log_gamma[:, None, None] * jnp.abs(idx[:, None] - idx[None, :]))
    q_decay = jnp.exp(log_gamma[:, None] * (idx + 1.0))
    k_decay = jnp.exp(log_gamma[:, None] * (CHUNK - 1.0 - idx))
    chunk_decay = jnp.exp(log_gamma * CHUNK)

    scores = jnp.einsum('bhncd,bhnkd->bhnck', qc, kc) * intra_decay[:, None]
    o_intra = jnp.einsum('bhnck,bhnke->bhnce', scores, vc)

    def step(state, xs):
        q_j, k_j, v_j = xs
        o_j = jnp.einsum('bhcd,bhde->bhce', q_j * q_decay[:, :, None], state)
        state = chunk_decay[:, None, None] * state + jnp.einsum(
            'bhcd,bhce->bhde', k_j * k_decay[:, :, None], v_j)
        return state, o_j

    xs = (qc.transpose(2, 0, 1, 3, 4), kc.transpose(2, 0, 1, 3, 4), vc.transpose(2, 0, 1, 3, 4))
    state0 = jnp.zeros((B, RET_HEADS, RET_QK_DIM, RET_V_DIM), jnp.float32)
    _, o_inter = lax.scan(step, state0, xs)

    o = o_intra + o_inter.transpose(1, 2, 0, 3, 4)
    o = o.transpose(0, 2, 3, 1, 4).reshape(B, S, RET_HEADS, RET_V_DIM)
    o = rms_norm_f32(o, gn_g.reshape(RET_HEADS, RET_V_DIM))
    o = o.reshape(B, S, RET_WIDTH) * jax.nn.silu(g.astype(jnp.float32))
    return o.astype(h.dtype) @ w_out


def chunked_attention_mixer(h, w_in, q_g, k_g, rel_table, w_out):
    B, S, _ = h.shape
    nc = S // CHUNK
    proj = h @ w_in
    q, k, v, g = jnp.split(proj, 4, axis=-1)
    q = rms_norm_f32(q.reshape(B, S, ATT_HEADS, ATT_HEAD_DIM), q_g)
    k = rms_norm_f32(k.reshape(B, S, ATT_HEADS, ATT_HEAD_DIM), k_g)
    v = v.reshape(B, S, ATT_HEADS, ATT_HEAD_DIM).astype(jnp.float32)

    pad = LEFT_CHUNKS * CHUNK
    k_pad = jnp.pad(k, ((0, 0), (pad, 0), (0, 0), (0, 0)))
    v_pad = jnp.pad(v, ((0, 0), (pad, 0), (0, 0), (0, 0)))

    qi = jnp.arange(CHUNK)
    kb = jnp.arange(BAND)
    rel = qi[:, None] + pad - kb[None, :]
    bias = rel_table.astype(jnp.float32)[:, jnp.clip(rel, -MAX_REL, MAX_REL) + MAX_REL]
    scale = ATT_HEAD_DIM ** -0.5

    q_chunks = q.reshape(B, nc, CHUNK, ATT_HEADS, ATT_HEAD_DIM).transpose(1, 0, 2, 3, 4)

    def one_chunk(args):
        j, q_j = args
        k_j = lax.dynamic_slice_in_dim(k_pad, j * CHUNK, BAND, axis=1)
        v_j = lax.dynamic_slice_in_dim(v_pad, j * CHUNK, BAND, axis=1)
        s = jnp.einsum('bqhd,bkhd->bhqk', q_j, k_j) * scale + bias
        valid = kb >= (LEFT_CHUNKS - j) * CHUNK
        s = jnp.where(valid, s, jnp.float32(-1e30))
        p = jax.nn.softmax(s, axis=-1)
        return jnp.einsum('bhqk,bkhd->bqhd', p, v_j)

    o = lax.map(one_chunk, (jnp.arange(nc), q_chunks))
    o = o.transpose(1, 0, 2, 3, 4).reshape(B, S, ATT_WIDTH)
    o = o * jax.nn.silu(g.astype(jnp.float32))
    return o.astype(h.dtype) @ w_out


def setup_inputs(seed: int = 0) -> dict:
    key = jax.random.key(seed)
    ks = jax.random.split(key, 16)
    D = D_MODEL
    x = jax.random.normal(ks[0], (BATCH, SEQ, D), jnp.float32)
    c = jax.random.normal(ks[1], (BATCH, D), jnp.float32)
    offsets = jax.random.randint(ks[2], (BATCH, 1), 0, 4096, dtype=jnp.int32)
    positions = offsets + jnp.arange(SEQ, dtype=jnp.int32)[None, :]
    norm_g = 1.0 + 0.1 * jax.random.normal(ks[3], (DEPTH, D), jnp.float32)
    ada_w = jax.random.normal(ks[4], (DEPTH, D, 3 * D), jnp.float32) * D ** -0.5
    ada_b = 0.02 * jax.random.normal(ks[5], (DEPTH, 3 * D), jnp.float32)
    ret_w_in = jax.random.normal(ks[6], (N_RET_LAYERS, D, 2 * RET_QK_WIDTH + 2 * RET_WIDTH), jnp.float32) * D ** -0.5
    ret_gn_g = 1.0 + 0.1 * jax.random.normal(ks[7], (N_RET_LAYERS, RET_WIDTH), jnp.float32)
    ret_w_out = jax.random.normal(ks[8], (N_RET_LAYERS, RET_WIDTH, D), jnp.float32) * RET_WIDTH ** -0.5
    att_w_in = jax.random.normal(ks[9], (N_ATT_LAYERS, D, 4 * ATT_WIDTH), jnp.float32) * D ** -0.5
    att_q_g = 1.0 + 0.1 * jax.random.normal(ks[10], (N_ATT_LAYERS, ATT_HEAD_DIM), jnp.float32)
    att_k_g = 1.0 + 0.1 * jax.random.normal(ks[11], (N_ATT_LAYERS, ATT_HEAD_DIM), jnp.float32)
    att_rel_bias = 0.5 * jax.random.normal(ks[12], (N_ATT_LAYERS, ATT_HEADS, 2 * MAX_REL + 1), jnp.float32)
    att_w_out = jax.random.normal(ks[13], (N_ATT_LAYERS, ATT_WIDTH, D), jnp.float32) * ATT_WIDTH ** -0.5
    return {"x": x, "c": c, "positions": positions, "norm_g": norm_g, "ada_w": ada_w,
            "ada_b": ada_b, "ret_w_in": ret_w_in, "ret_gn_g": ret_gn_g, "ret_w_out": ret_w_out,
            "att_w_in": att_w_in, "att_q_g": att_q_g, "att_k_g": att_k_g,
            "att_rel_bias": att_rel_bias, "att_w_out": att_w_out}


def reference(x, c, positions, norm_g, ada_w, ada_b, ret_w_in, ret_gn_g, ret_w_out,
              att_w_in, att_q_g, att_k_g, att_rel_bias, att_w_out):
    cond = jax.nn.silu(c.astype(jnp.float32))
    for i in range(DEPTH):
        mod = cond @ ada_w[i].astype(jnp.float32) + ada_b[i].astype(jnp.float32)
        shift, scale, gate = jnp.split(mod[:, None, :], 3, axis=-1)
        h = (rms_norm_f32(x, norm_g[i]) * (1.0 + scale) + shift).astype(x.dtype)
        j = i // N_MIXERS
        if i % N_MIXERS == 0:
            out = retention_mixer(h, positions, ret_w_in[j], ret_gn_g[j], ret_w_out[j])
        else:
            out = chunked_attention_mixer(h, att_w_in[j], att_q_g[j], att_k_g[j],
                                          att_rel_bias[j], att_w_out[j])
        x = x + (gate * out.astype(jnp.float32)).astype(x.dtype)
    return x
```

```python
import functools
import math

import jax
import jax.numpy as jnp
import numpy as np
from jax import lax
from jax.experimental import pallas as pl
from jax.experimental.pallas import tpu as pltpu

EPS = 1e-6
CHUNK = 64

RET_HEADS = 4
RET_QK_DIM = 256
RET_V_DIM = 512
ROPE_BASE = 10000.0

ATT_HEADS = 16
ATT_HEAD_DIM = 128
LEFT_CHUNKS = 8
MAX_REL = 2 * CHUNK
NEG_INF = -1e30

PROJ_ROWS = 512
RET_BLOCK = 256
RET_STEP_ROWS = 1024
ATT_QBLOCK = 256
ATT_STEP_ROWS = 1024
ATT_KWIN = ATT_QBLOCK + LEFT_CHUNKS * CHUNK

VMEM_LIMIT_BYTES = 56 * 1024 * 1024

BF16 = jnp.bfloat16
F32 = jnp.float32


def _resident(shape):
    return pl.BlockSpec(shape, lambda *_: (0,) * len(shape), pipeline_mode=pl.Buffered(1))


def _params(semantics):
    return pltpu.CompilerParams(dimension_semantics=semantics, vmem_limit_bytes=VMEM_LIMIT_BYTES)


def _adaln_kernel(c_ref, w_ref, b_ref, o_ref):
    c = c_ref[...]
    cond = c * (1.0 / (1.0 + jnp.exp(-c)))
    o_ref[0] = jnp.dot(cond, w_ref[0], preferred_element_type=F32) + b_ref[0]


def _adaln(c, ada_w, ada_b):
    depth, d, d3 = ada_w.shape
    b = c.shape[0]
    col = 1024
    return pl.pallas_call(
        _adaln_kernel,
        out_shape=jax.ShapeDtypeStruct((depth, b, d3), F32),
        grid=(depth, d3 // col),
        in_specs=[
            pl.BlockSpec((b, d), lambda i, j: (0, 0)),
            pl.BlockSpec((1, d, col), lambda i, j: (i, 0, j)),
            pl.BlockSpec((1, 1, col), lambda i, j: (i, 0, j)),
        ],
        out_specs=pl.BlockSpec((1, b, col), lambda i, j: (i, 0, j)),
        compiler_params=_params(("arbitrary", "arbitrary")),
        name="adaln",
    )(c, ada_w, ada_b.reshape(depth, 1, d3))


def _modulated_norm(x_ref, g_ref, shift_ref, scale_ref):
    x = x_ref[...]
    ms = jnp.mean(x * x, axis=-1, keepdims=True)
    h = x * lax.rsqrt(ms + EPS) * g_ref[...]
    h = h * (1.0 + scale_ref[0]) + shift_ref[0]
    return h.astype(BF16)


def _silu(t):
    return t * (1.0 / (1.0 + jnp.exp(-t)))


def _ret_proj_kernel(x_ref, pos_ref, freq_ref, g_ref, shift_ref, scale_ref, w_ref,
                     q_ref, k_ref, v_ref, sg_ref, h_ref):
    h_ref[...] = _modulated_norm(x_ref, g_ref, shift_ref, scale_ref)
    ang = pos_ref[...].astype(F32) * freq_ref[...]
    cos = jnp.cos(ang)
    sin = jnp.sin(ang)
    half = RET_QK_DIM // 2
    qk_width = RET_HEADS * RET_QK_DIM

    def proj(col, width):
        return jnp.dot(h_ref[...], w_ref[:, col:col + width], preferred_element_type=F32)

    for out_ref, base in ((q_ref, 0), (k_ref, qk_width)):
        for hh in range(RET_HEADS):
            acc = proj(base + hh * RET_QK_DIM, RET_QK_DIM)
            t1 = acc[:, :half]
            t2 = acc[:, half:]
            out_ref[:, hh * RET_QK_DIM:hh * RET_QK_DIM + half] = (t1 * cos - t2 * sin).astype(BF16)
            out_ref[:, hh * RET_QK_DIM + half:(hh + 1) * RET_QK_DIM] = (t1 * sin + t2 * cos).astype(BF16)
    v_width = RET_HEADS * RET_V_DIM
    for c in range(0, v_width, RET_V_DIM):
        v_ref[:, c:c + RET_V_DIM] = proj(2 * qk_width + c, RET_V_DIM).astype(BF16)
    for c in range(0, v_width, RET_V_DIM):
        sg_ref[:, c:c + RET_V_DIM] = _silu(proj(2 * qk_width + v_width + c, RET_V_DIM)).astype(BF16)


def _ret_proj(x2, pos2, freq, norm_g, shift, scale, w_bf16, seq):
    n, d = x2.shape
    tm = PROJ_ROWS
    assert n % tm == 0 and seq % tm == 0
    per_seq = seq // tm
    qk_width = RET_HEADS * RET_QK_DIM
    v_width = RET_HEADS * RET_V_DIM
    row = lambda i: (i, 0)
    bat = lambda i: (i // per_seq, 0, 0)
    return pl.pallas_call(
        _ret_proj_kernel,
        out_shape=(
            jax.ShapeDtypeStruct((n, qk_width), BF16),
            jax.ShapeDtypeStruct((n, qk_width), BF16),
            jax.ShapeDtypeStruct((n, v_width), BF16),
            jax.ShapeDtypeStruct((n, v_width), BF16),
        ),
        grid=(n // tm,),
        in_specs=[
            pl.BlockSpec((tm, d), row),
            pl.BlockSpec((tm, 1), row),
            _resident((1, RET_QK_DIM // 2)),
            _resident((1, d)),
            pl.BlockSpec((1, 1, d), bat),
            pl.BlockSpec((1, 1, d), bat),
            _resident(w_bf16.shape),
        ],
        out_specs=(
            pl.BlockSpec((tm, qk_width), row),
            pl.BlockSpec((tm, qk_width), row),
            pl.BlockSpec((tm, v_width), row),
            pl.BlockSpec((tm, v_width), row),
        ),
        scratch_shapes=[pltpu.VMEM((tm, d), BF16)],
        compiler_params=_params(("arbitrary",)),
        name="ret_proj",
    )(x2, pos2, freq, norm_g, shift, scale, w_bf16)


def _retention_kernel(q_ref, k_ref, v_ref, sg_ref, dmask_ref, qd_ref, kd_ref, bd_ref, gn_ref,
                      o_ref, state_ref):
    @pl.when(pl.program_id(2) == 0)
    def _():
        state_ref[...] = jnp.zeros_like(state_ref)

    dmask = dmask_ref[0]
    qd = qd_ref[0]
    kd = kd_ref[0]
    decay = bd_ref[0]
    gn = gn_ref[0]
    for blk in range(RET_STEP_ROWS // RET_BLOCK):
        rows = pl.ds(blk * RET_BLOCK, RET_BLOCK)
        q = q_ref[rows, :]
        k = k_ref[rows, :]
        v = v_ref[rows, :]
        state = state_ref[...]
        scores = lax.dot_general(q, k, (((1,), (1,)), ((), ())), preferred_element_type=F32)
        o = jnp.dot((scores * dmask).astype(BF16), v, preferred_element_type=F32)
        qs = (q.astype(F32) * qd).astype(BF16)
        o = o + jnp.dot(qs, state.astype(BF16), preferred_element_type=F32)
        ks = (k.astype(F32) * kd).astype(BF16)
        state_ref[...] = decay * state + lax.dot_general(
            ks, v, (((0,), (0,)), ((), ())), preferred_element_type=F32)
        ms = jnp.mean(o * o, axis=-1, keepdims=True)
        o = o * lax.rsqrt(ms + EPS) * gn
        o_ref[rows, :] = (o * sg_ref[rows, :].astype(F32)).astype(BF16)


def _retention_tables():
    log_gamma = jnp.log(1.0 - 2.0 ** (-5.0 - jnp.arange(RET_HEADS, dtype=F32)))
    t = jnp.arange(RET_BLOCK, dtype=F32)
    dist = t[:, None] - t[None, :]
    chunk = jnp.arange(RET_BLOCK) // CHUNK
    visible = chunk[None, :] <= chunk[:, None]
    k_scale = RET_QK_DIM ** -0.5
    dmask = jnp.where(visible[None], jnp.exp(log_gamma[:, None, None] * jnp.abs(dist)[None]), 0.0) * k_scale
    qd = jnp.exp(log_gamma[:, None] * (t + 1.0))
    kd = jnp.exp(log_gamma[:, None] * (RET_BLOCK - 1.0 - t)) * k_scale
    qd = jnp.broadcast_to(qd[:, :, None], (RET_HEADS, RET_BLOCK, RET_QK_DIM))
    kd = jnp.broadcast_to(kd[:, :, None], (RET_HEADS, RET_BLOCK, RET_QK_DIM))
    bd = jnp.broadcast_to(jnp.exp(log_gamma * RET_BLOCK)[:, None, None], (RET_HEADS, 1, RET_V_DIM))
    return dmask.astype(F32), qd.astype(F32), kd.astype(F32), bd.astype(F32)


def _retention(q, k, v, sg, gn_g, batch, seq):
    n = q.shape[0]
    rows = RET_STEP_ROWS
    assert seq % rows == 0
    per_seq = seq // rows
    dmask, qd, kd, bd = _retention_tables()
    row_head = lambda b, h, j: (b * per_seq + j, h)
    head_tab = lambda b, h, j: (h, 0, 0)
    return pl.pallas_call(
        _retention_kernel,
        out_shape=jax.ShapeDtypeStruct((n, RET_HEADS * RET_V_DIM), BF16),
        grid=(batch, RET_HEADS, per_seq),
        in_specs=[
            pl.BlockSpec((rows, RET_QK_DIM), row_head),
            pl.BlockSpec((rows, RET_QK_DIM), row_head),
            pl.BlockSpec((rows, RET_V_DIM), row_head),
            pl.BlockSpec((rows, RET_V_DIM), row_head),
            pl.BlockSpec((1, RET_BLOCK, RET_BLOCK), head_tab),
            pl.BlockSpec((1, RET_BLOCK, RET_QK_DIM), head_tab),
            pl.BlockSpec((1, RET_BLOCK, RET_QK_DIM), head_tab),
            pl.BlockSpec((1, 1, RET_V_DIM), head_tab),
            pl.BlockSpec((1, 1, RET_V_DIM), head_tab),
        ],
        out_specs=pl.BlockSpec((rows, RET_V_DIM), row_head),
        scratch_shapes=[pltpu.VMEM((RET_QK_DIM, RET_V_DIM), F32)],
        compiler_params=_params(("arbitrary", "arbitrary", "arbitrary")),
        name="retention",
    )(q, k, v, sg, dmask, qd, kd, bd, gn_g.reshape(RET_HEADS, 1, RET_V_DIM))


def _out_proj_kernel(o_ref, w_ref, x_ref, gate_ref, y_ref):
    out = jnp.dot(o_ref[...], w_ref[...], preferred_element_type=F32)
    y_ref[...] = x_ref[...] + gate_ref[0] * out


def _out_proj(o, w_bf16, x2, gate, seq):
    n, d = x2.shape
    width = o.shape[1]
    tm = PROJ_ROWS
    per_seq = seq // tm
    row = lambda i: (i, 0)
    return pl.pallas_call(
        _out_proj_kernel,
        out_shape=jax.ShapeDtypeStruct((n, d), F32),
        grid=(n // tm,),
        in_specs=[
            pl.BlockSpec((tm, width), row),
            _resident(w_bf16.shape),
            pl.BlockSpec((tm, d), row),
            pl.BlockSpec((1, 1, d), lambda i: (i // per_seq, 0, 0)),
        ],
        out_specs=pl.BlockSpec((tm, d), row),
        compiler_params=_params(("arbitrary",)),
        name="out_proj",
    )(o, w_bf16, x2, gate)


def _att_proj_kernel(x_ref, g_ref, shift_ref, scale_ref, w_ref, qg_ref, kg_ref,
                     q_ref, k_ref, v_ref, sg_ref, h_ref):
    h_ref[...] = _modulated_norm(x_ref, g_ref, shift_ref, scale_ref)
    width = ATT_HEADS * ATT_HEAD_DIM
    chunk = 512

    def proj(col, w):
        return jnp.dot(h_ref[...], w_ref[:, col:col + w], preferred_element_type=F32)

    for out_ref, gain_ref, base, post in ((q_ref, qg_ref, 0, ATT_HEAD_DIM ** -0.5), (k_ref, kg_ref, width, 1.0)):
        gain = gain_ref[...] * post
        for c in range(0, width, chunk):
            acc = proj(base + c, chunk)
            for hh in range(chunk // ATT_HEAD_DIM):
                a = acc[:, hh * ATT_HEAD_DIM:(hh + 1) * ATT_HEAD_DIM]
                ms = jnp.mean(a * a, axis=-1, keepdims=True)
                out_ref[:, c + hh * ATT_HEAD_DIM:c + (hh + 1) * ATT_HEAD_DIM] = (
                    a * lax.rsqrt(ms + EPS) * gain).astype(BF16)
    for c in range(0, width, chunk):
        v_ref[:, c:c + chunk] = proj(2 * width + c, chunk).astype(BF16)
    for c in range(0, width, chunk):
        sg_ref[:, c:c + chunk] = _silu(proj(3 * width + c, chunk)).astype(BF16)


def _att_proj(x2, norm_g, shift, scale, w_bf16, q_g, k_g, seq):
    n, d = x2.shape
    tm = PROJ_ROWS
    per_seq = seq // tm
    width = ATT_HEADS * ATT_HEAD_DIM
    row = lambda i: (i, 0)
    bat = lambda i: (i // per_seq, 0, 0)
    out = jax.ShapeDtypeStruct((n, width), BF16)
    return pl.pallas_call(
        _att_proj_kernel,
        out_shape=(out, out, out, out),
        grid=(n // tm,),
        in_specs=[
            pl.BlockSpec((tm, d), row),
            _resident((1, d)),
            pl.BlockSpec((1, 1, d), bat),
            pl.BlockSpec((1, 1, d), bat),
            _resident(w_bf16.shape),
            _resident((1, ATT_HEAD_DIM)),
            _resident((1, ATT_HEAD_DIM)),
        ],
        out_specs=tuple(pl.BlockSpec((tm, width), row) for _ in range(4)),
        scratch_shapes=[pltpu.VMEM((tm, d), BF16)],
        compiler_params=_params(("arbitrary",)),
        name="att_proj",
    )(x2, norm_g, shift, scale, w_bf16, q_g, k_g)


def _attend(q, k, v, bias, sg):
    s = lax.dot_general(q, k, (((1,), (1,)), ((), ())), preferred_element_type=F32) + bias
    m = jnp.max(s, axis=-1, keepdims=True)
    p = jnp.exp(s - m)
    l = jnp.sum(p, axis=-1, keepdims=True)
    o = jnp.dot(p.astype(BF16), v, preferred_element_type=F32) / l
    return (o * sg.astype(F32)).astype(BF16)


def _attention_kernel(q_ref, k_ref, v_ref, sg_ref, bias_ref, o_ref):
    step = pl.program_id(2)
    blocks = ATT_STEP_ROWS // ATT_QBLOCK
    left = ATT_KWIN - ATT_QBLOCK

    def run(blk, key_start, n_keys, bias_from):
        rows = pl.ds(blk * ATT_QBLOCK, ATT_QBLOCK)
        keys = pl.ds(key_start, n_keys)
        for hh in range(2):
            cols = slice(hh * ATT_HEAD_DIM, (hh + 1) * ATT_HEAD_DIM)
            o_ref[rows, cols] = _attend(
                q_ref[rows, cols], k_ref[keys, cols], v_ref[keys, cols],
                bias_ref[hh, :, bias_from:], sg_ref[rows, cols])

    @pl.when(step == 0)
    def _():
        for blk in range(blocks):
            first = max(blk * ATT_QBLOCK - left, 0)
            n_keys = (blk + 1) * ATT_QBLOCK - first
            run(blk, first, n_keys, ATT_KWIN - n_keys)

    @pl.when(step > 0)
    def _():
        base = step * ATT_STEP_ROWS - left
        for blk in range(blocks):
            run(blk, pl.multiple_of(base + blk * ATT_QBLOCK, ATT_QBLOCK), ATT_KWIN, 0)


def _attention_bias(rel_table):
    left = ATT_KWIN - ATT_QBLOCK
    qi = jnp.arange(ATT_QBLOCK)
    kj = jnp.arange(ATT_KWIN)
    rel = qi[:, None] + left - kj[None, :]
    bias = rel_table.astype(F32)[:, jnp.clip(rel, -MAX_REL, MAX_REL) + MAX_REL]
    q_chunk = (qi + left) // CHUNK
    k_chunk = kj // CHUNK
    band = (k_chunk[None, :] <= q_chunk[:, None]) & (k_chunk[None, :] >= q_chunk[:, None] - LEFT_CHUNKS)
    return jnp.where(band[None], bias, NEG_INF)


def _attention(q, k, v, sg, rel_table, batch, seq):
    n, width = q.shape
    rows = ATT_STEP_ROWS
    assert seq % rows == 0
    per_seq = seq // rows
    bias = _attention_bias(rel_table)
    pair = 2 * ATT_HEAD_DIM
    row_pair = lambda b, h, j: (b * per_seq + j, h)
    seq_pair = lambda b, h, j: (b, h)
    return pl.pallas_call(
        _attention_kernel,
        out_shape=jax.ShapeDtypeStruct((n, width), BF16),
        grid=(batch, ATT_HEADS // 2, per_seq),
        in_specs=[
            pl.BlockSpec((rows, pair), row_pair),
            pl.BlockSpec((seq, pair), seq_pair),
            pl.BlockSpec((seq, pair), seq_pair),
            pl.BlockSpec((rows, pair), row_pair),
            pl.BlockSpec((2, ATT_QBLOCK, ATT_KWIN), lambda b, h, j: (h, 0, 0)),
        ],
        out_specs=pl.BlockSpec((rows, pair), row_pair),
        compiler_params=_params(("arbitrary", "arbitrary", "arbitrary")),
        name="attention",
    )(q, k, v, sg, bias)


def kernel(x, c, positions, norm_g, ada_w, ada_b, ret_w_in, ret_gn_g, ret_w_out,
           att_w_in, att_q_g, att_k_g, att_rel_bias, att_w_out):
    batch, seq, d = x.shape
    n = batch * seq
    mod = _adaln(c, ada_w, ada_b)
    shift = mod[:, :, :d].reshape(-1, batch, 1, d)
    scale = mod[:, :, d:2 * d].reshape(-1, batch, 1, d)
    gate = mod[:, :, 2 * d:].reshape(-1, batch, 1, d)

    x2 = x.reshape(n, d)
    pos2 = positions.reshape(n, 1)
    freq = (1.0 / (ROPE_BASE ** (jnp.arange(0, RET_QK_DIM, 2, dtype=F32) / RET_QK_DIM))).reshape(1, -1)

    q, k, v, sg = _ret_proj(x2, pos2, freq, norm_g[0].reshape(1, d), shift[0], scale[0],
                            ret_w_in[0].astype(BF16), seq)
    o = _retention(q, k, v, sg, ret_gn_g[0], batch, seq)
    x2 = _out_proj(o, ret_w_out[0].astype(BF16), x2, gate[0], seq)

    q, k, v, sg = _att_proj(x2, norm_g[1].reshape(1, d), shift[1], scale[1],
                            att_w_in[0].astype(BF16), att_q_g[0].reshape(1, -1),
                            att_k_g[0].reshape(1, -1), seq)
    o = _attention(q, k, v, sg, att_rel_bias[0], batch, seq)
    x2 = _out_proj(o, att_w_out[0].astype(BF16), x2, gate[1], seq)
    return x2.reshape(batch, seq, d)
```

```python
import functools
import itertools
import math

import jax
import jax.numpy as jnp
import numpy as np
from jax import lax
from jax.experimental import pallas as pl
from jax.experimental.pallas import tpu as pltpu

EPS = 1e-6
CHUNK = 64

RET_HEADS = 4
RET_QK_DIM = 256
RET_V_DIM = 512
ROPE_BASE = 10000.0

ATT_HEADS = 16
ATT_HEAD_DIM = 128
LEFT_CHUNKS = 8
MAX_REL = 2 * CHUNK
NEG_INF = -1e30

PROJ_ROWS = 512
RET_BLOCK = 256
RET_STEP_ROWS = 1024
ATT_QBLOCK = 256
ATT_STEP_ROWS = 1024
ATT_KWIN = ATT_QBLOCK + LEFT_CHUNKS * CHUNK
ATT_BIAS_ROW = 1024
LOG2E = math.log2(math.e)

VMEM_LIMIT_BYTES = 56 * 1024 * 1024

BF16 = jnp.bfloat16
F32 = jnp.float32


def _resident(shape):
    return pl.BlockSpec(shape, lambda *_: (0,) * len(shape), pipeline_mode=pl.Buffered(1))


def _params(semantics):
    return pltpu.CompilerParams(dimension_semantics=semantics, vmem_limit_bytes=VMEM_LIMIT_BYTES)


def _adaln_kernel(c_ref, w_ref, b_ref, o_ref):
    c = c_ref[...]
    cond = c * (1.0 / (1.0 + jnp.exp(-c)))
    o_ref[0] = jnp.dot(cond, w_ref[0], preferred_element_type=F32) + b_ref[0]


def _adaln(c, ada_w, ada_b):
    depth, d, d3 = ada_w.shape
    b = c.shape[0]
    col = 1024
    return pl.pallas_call(
        _adaln_kernel,
        out_shape=jax.ShapeDtypeStruct((depth, b, d3), F32),
        grid=(depth, d3 // col),
        in_specs=[
            pl.BlockSpec((b, d), lambda i, j: (0, 0)),
            pl.BlockSpec((1, d, col), lambda i, j: (i, 0, j)),
            pl.BlockSpec((1, 1, col), lambda i, j: (i, 0, j)),
        ],
        out_specs=pl.BlockSpec((1, b, col), lambda i, j: (i, 0, j)),
        compiler_params=_params(("arbitrary", "arbitrary")),
        name="adaln",
    )(c, ada_w, ada_b.reshape(depth, 1, d3))


PIECES_AFTER_BUSY_DOT = 1
PIECES_AFTER_CAST_DOT = 5
PROLOGUE_PIECES = 12 * PIECES_AFTER_BUSY_DOT + 4 * PIECES_AFTER_CAST_DOT


def _modulated_norm(x, g_ref, shift_ref, scale_ref):
    ms = jnp.mean(x * x, axis=-1, keepdims=True)
    h = x * lax.rsqrt(ms + EPS) * g_ref[...]
    h = h * (1.0 + scale_ref[0]) + shift_ref[0]
    return h.astype(BF16)


def _piece_rows(piece, tile_rows):
    n = tile_rows // PROLOGUE_PIECES
    return slice(piece * n, (piece + 1) * n)


def _silu(t):
    return t * (1.0 / (1.0 + jnp.exp(-t)))


def _zero_from(*tiles):
    acc = None
    for t in tiles:
        bits = pltpu.bitcast(t, jnp.uint32)
        for r in range(0, bits.shape[0], 8):
            for c in range(0, bits.shape[1], 128):
                part = bits[r:r + 8, c:c + 128]
                acc = part if acc is None else acc | part
    return ((acc >> 16) >> 16).astype(F32)


def _dot_then_pieces(h_ref, w_ref, col, width, pieces, count, next_piece):
    acc = jnp.dot(h_ref[...], w_ref[:, col:col + width], preferred_element_type=F32)
    for piece in itertools.islice(pieces, count):
        top = jnp.concatenate([acc[:8, :128] + next_piece(piece), acc[:8, 128:]], axis=1)
        acc = jnp.concatenate([top, acc[8:]], axis=0)
    return acc


def _ret_prologue(piece, x_ref, pos_ref, freq_ref, g_ref, shift_ref, scale_ref, h_ref, cos_ref, sin_ref):
    rows = _piece_rows(piece, x_ref.shape[0])
    h = _modulated_norm(x_ref[rows, :], g_ref, shift_ref, scale_ref)
    h_ref[rows, :] = h
    ang = pos_ref[rows, :].astype(F32) * freq_ref[...]
    cos = jnp.cos(ang)
    sin = jnp.sin(ang)
    cos_ref[rows, :] = cos
    sin_ref[rows, :] = sin
    return _zero_from(h, cos, sin)


def _ret_project(h_ref, cos_ref, sin_ref, w_ref, q_ref, k_ref, v_ref, sg_ref, next_piece):
    cos = cos_ref[...]
    sin = sin_ref[...]
    half = RET_QK_DIM // 2
    qk_width = RET_HEADS * RET_QK_DIM
    v_width = RET_HEADS * RET_V_DIM
    pieces = iter(range(PROLOGUE_PIECES))
    proj = functools.partial(_dot_then_pieces, h_ref, w_ref, pieces=pieces, next_piece=next_piece)

    for out_ref, base in ((q_ref, 0), (k_ref, qk_width)):
        for hh in range(RET_HEADS):
            acc = proj(base + hh * RET_QK_DIM, RET_QK_DIM, count=PIECES_AFTER_BUSY_DOT)
            t1 = acc[:, :half]
            t2 = acc[:, half:]
            out_ref[:, hh * RET_QK_DIM:hh * RET_QK_DIM + half] = (t1 * cos - t2 * sin).astype(BF16)
            out_ref[:, hh * RET_QK_DIM + half:(hh + 1) * RET_QK_DIM] = (t1 * sin + t2 * cos).astype(BF16)
    for c in range(0, v_width, RET_V_DIM):
        acc = proj(2 * qk_width + v_width + c, RET_V_DIM, count=PIECES_AFTER_BUSY_DOT)
        sg_ref[:, c:c + RET_V_DIM] = _silu(acc).astype(BF16)
    for c in range(0, v_width, RET_V_DIM):
        acc = proj(2 * qk_width + c, RET_V_DIM, count=PIECES_AFTER_CAST_DOT)
        v_ref[:, c:c + RET_V_DIM] = acc.astype(BF16)
    assert next(pieces, None) is None


def _ret_proj_kernel(x0_ref, xn_ref, pos0_ref, posn_ref, freq_ref, g_ref,
                     shift0_ref, scale0_ref, shiftn_ref, scalen_ref, w_ref,
                     q_ref, k_ref, v_ref, sg_ref,
                     h_a, cos_a, sin_a, h_b, cos_b, sin_b):
    step = pl.program_id(0)

    @pl.when(step == 0)
    def _():
        for piece in range(PROLOGUE_PIECES):
            _ret_prologue(piece, x0_ref, pos0_ref, freq_ref, g_ref, shift0_ref, scale0_ref, h_a, cos_a, sin_a)

    def run(cur, nxt):
        next_piece = functools.partial(_ret_prologue, x_ref=xn_ref, pos_ref=posn_ref, freq_ref=freq_ref,
                                       g_ref=g_ref, shift_ref=shiftn_ref, scale_ref=scalen_ref,
                                       h_ref=nxt[0], cos_ref=nxt[1], sin_ref=nxt[2])
        _ret_project(*cur, w_ref, q_ref, k_ref, v_ref, sg_ref, next_piece)

    set_a = (h_a, cos_a, sin_a)
    set_b = (h_b, cos_b, sin_b)

    @pl.when(step % 2 == 0)
    def _():
        run(set_a, set_b)

    @pl.when(step % 2 == 1)
    def _():
        run(set_b, set_a)


def _ret_proj(x2, pos2, freq, norm_g, shift, scale, w_bf16, seq):
    n, d = x2.shape
    tm = PROJ_ROWS
    assert n % tm == 0 and seq % tm == 0
    per_seq = seq // tm
    last = n // tm - 1
    qk_width = RET_HEADS * RET_QK_DIM
    v_width = RET_HEADS * RET_V_DIM
    half = RET_QK_DIM // 2
    row = lambda i: (i, 0)
    nxt = lambda i: (jnp.minimum(i + 1, last), 0)
    nxt_bat = lambda i: (jnp.minimum(i + 1, last) // per_seq, 0, 0)
    return pl.pallas_call(
        _ret_proj_kernel,
        out_shape=(
            jax.ShapeDtypeStruct((n, qk_width), BF16),
            jax.ShapeDtypeStruct((n, qk_width), BF16),
            jax.ShapeDtypeStruct((n, v_width), BF16),
            jax.ShapeDtypeStruct((n, v_width), BF16),
        ),
        grid=(n // tm,),
        in_specs=[
            _resident((tm, d)),
            pl.BlockSpec((tm, d), nxt),
            _resident((tm, 1)),
            pl.BlockSpec((tm, 1), nxt),
            _resident((1, half)),
            _resident((1, d)),
            _resident((1, 1, d)),
            _resident((1, 1, d)),
            pl.BlockSpec((1, 1, d), nxt_bat),
            pl.BlockSpec((1, 1, d), nxt_bat),
            _resident(w_bf16.shape),
        ],
        out_specs=(
            pl.BlockSpec((tm, qk_width), row),
            pl.BlockSpec((tm, qk_width), row),
            pl.BlockSpec((tm, v_width), row),
            pl.BlockSpec((tm, v_width), row),
        ),
        scratch_shapes=[pltpu.VMEM((tm, d), BF16), pltpu.VMEM((tm, half), F32), pltpu.VMEM((tm, half), F32),
                        pltpu.VMEM((tm, d), BF16), pltpu.VMEM((tm, half), F32), pltpu.VMEM((tm, half), F32)],
        compiler_params=_params(("arbitrary",)),
        name="ret_proj",
    )(x2, x2, pos2, pos2, freq, norm_g, shift, scale, shift, scale, w_bf16)


def _retention_kernel(q_ref, k_ref, v_ref, sg_ref, dmask_ref, qd_ref, kd_ref, bd_ref, gn_ref,
                      o_ref, state_ref):
    @pl.when(pl.program_id(2) == 0)
    def _():
        state_ref[...] = jnp.zeros_like(state_ref)

    dmask = dmask_ref[0]
    qd = qd_ref[0]
    kd = kd_ref[0]
    decay = bd_ref[0]
    gn = gn_ref[0]
    for blk in range(RET_STEP_ROWS // RET_BLOCK):
        rows = pl.ds(blk * RET_BLOCK, RET_BLOCK)
        q = q_ref[rows, :]
        k = k_ref[rows, :]
        v = v_ref[rows, :]
        state = state_ref[...]
        scores = lax.dot_general(q, k, (((1,), (1,)), ((), ())), preferred_element_type=F32)
        o = jnp.dot((scores * dmask).astype(BF16), v, preferred_element_type=F32)
        qs = (q.astype(F32) * qd).astype(BF16)
        o = o + jnp.dot(qs, state.astype(BF16), preferred_element_type=F32)
        ks = (k.astype(F32) * kd).astype(BF16)
        state_ref[...] = decay * state + lax.dot_general(
            ks, v, (((0,), (0,)), ((), ())), preferred_element_type=F32)
        ms = jnp.mean(o * o, axis=-1, keepdims=True)
        o = o * lax.rsqrt(ms + EPS) * gn
        o_ref[rows, :] = (o * sg_ref[rows, :].astype(F32)).astype(BF16)


def _retention_tables():
    log_gamma = jnp.log(1.0 - 2.0 ** (-5.0 - jnp.arange(RET_HEADS, dtype=F32)))
    t = jnp.arange(RET_BLOCK, dtype=F32)
    dist = t[:, None] - t[None, :]
    chunk = jnp.arange(RET_BLOCK) // CHUNK
    visible = chunk[None, :] <= chunk[:, None]
    k_scale = RET_QK_DIM ** -0.5
    dmask = jnp.where(visible[None], jnp.exp(log_gamma[:, None, None] * jnp.abs(dist)[None]), 0.0) * k_scale
    qd = jnp.exp(log_gamma[:, None] * (t + 1.0))
    kd = jnp.exp(log_gamma[:, None] * (RET_BLOCK - 1.0 - t)) * k_scale
    qd = jnp.broadcast_to(qd[:, :, None], (RET_HEADS, RET_BLOCK, RET_QK_DIM))
    kd = jnp.broadcast_to(kd[:, :, None], (RET_HEADS, RET_BLOCK, RET_QK_DIM))
    bd = jnp.broadcast_to(jnp.exp(log_gamma * RET_BLOCK)[:, None, None], (RET_HEADS, 1, RET_V_DIM))
    return dmask.astype(F32), qd.astype(F32), kd.astype(F32), bd.astype(F32)


def _retention(q, k, v, sg, gn_g, batch, seq):
    n = q.shape[0]
    rows = RET_STEP_ROWS
    assert seq % rows == 0
    per_seq = seq // rows
    dmask, qd, kd, bd = _retention_tables()
    row_head = lambda b, h, j: (b * per_seq + j, h)
    head_tab = lambda b, h, j: (h, 0, 0)
    return pl.pallas_call(
        _retention_kernel,
        out_shape=jax.ShapeDtypeStruct((n, RET_HEADS * RET_V_DIM), BF16),
        grid=(batch, RET_HEADS, per_seq),
        in_specs=[
            pl.BlockSpec((rows, RET_QK_DIM), row_head),
            pl.BlockSpec((rows, RET_QK_DIM), row_head),
            pl.BlockSpec((rows, RET_V_DIM), row_head),
            pl.BlockSpec((rows, RET_V_DIM), row_head),
            pl.BlockSpec((1, RET_BLOCK, RET_BLOCK), head_tab),
            pl.BlockSpec((1, RET_BLOCK, RET_QK_DIM), head_tab),
            pl.BlockSpec((1, RET_BLOCK, RET_QK_DIM), head_tab),
            pl.BlockSpec((1, 1, RET_V_DIM), head_tab),
            pl.BlockSpec((1, 1, RET_V_DIM), head_tab),
        ],
        out_specs=pl.BlockSpec((rows, RET_V_DIM), row_head),
        scratch_shapes=[pltpu.VMEM((RET_QK_DIM, RET_V_DIM), F32)],
        compiler_params=_params(("arbitrary", "arbitrary", "arbitrary")),
        name="retention",
    )(q, k, v, sg, dmask, qd, kd, bd, gn_g.reshape(RET_HEADS, 1, RET_V_DIM))


def _out_proj_kernel(o_ref, w_ref, x_ref, gate_ref, y_ref):
    out = jnp.dot(o_ref[...], w_ref[...], preferred_element_type=F32)
    y_ref[...] = x_ref[...] + gate_ref[0] * out


def _out_proj(o, w_bf16, x2, gate, seq):
    n, d = x2.shape
    width = o.shape[1]
    tm = PROJ_ROWS
    per_seq = seq // tm
    row = lambda i: (i, 0)
    return pl.pallas_call(
        _out_proj_kernel,
        out_shape=jax.ShapeDtypeStruct((n, d), F32),
        grid=(n // tm,),
        in_specs=[
            pl.BlockSpec((tm, width), row),
            _resident(w_bf16.shape),
            pl.BlockSpec((tm, d), row),
            pl.BlockSpec((1, 1, d), lambda i: (i // per_seq, 0, 0)),
        ],
        out_specs=pl.BlockSpec((tm, d), row),
        compiler_params=_params(("arbitrary",)),
        name="out_proj",
    )(o, w_bf16, x2, gate)


def _att_prologue(piece, x_ref, g_ref, shift_ref, scale_ref, h_ref):
    rows = _piece_rows(piece, x_ref.shape[0])
    h = _modulated_norm(x_ref[rows, :], g_ref, shift_ref, scale_ref)
    h_ref[rows, :] = h
    return _zero_from(h)


def _att_project(h_ref, w_ref, qg_ref, kg_ref, q_ref, k_ref, v_ref, sg_ref, next_piece):
    width = ATT_HEADS * ATT_HEAD_DIM
    chunk = 512
    pieces = iter(range(PROLOGUE_PIECES))
    proj = functools.partial(_dot_then_pieces, h_ref, w_ref, pieces=pieces, next_piece=next_piece)

    for out_ref, gain_ref, base, post in ((q_ref, qg_ref, 0, ATT_HEAD_DIM ** -0.5 * LOG2E), (k_ref, kg_ref, width, 1.0)):
        gain = gain_ref[...] * post
        for c in range(0, width, chunk):
            acc = proj(base + c, chunk, count=PIECES_AFTER_BUSY_DOT)
            for hh in range(chunk // ATT_HEAD_DIM):
                a = acc[:, hh * ATT_HEAD_DIM:(hh + 1) * ATT_HEAD_DIM]
                ms = jnp.mean(a * a, axis=-1, keepdims=True)
                out_ref[:, c + hh * ATT_HEAD_DIM:c + (hh + 1) * ATT_HEAD_DIM] = (
                    a * lax.rsqrt(ms + EPS) * gain).astype(BF16)
    for c in range(0, width, chunk):
        sg_ref[:, c:c + chunk] = _silu(proj(3 * width + c, chunk, count=PIECES_AFTER_BUSY_DOT)).astype(BF16)
    for c in range(0, width, chunk):
        v_ref[:, c:c + chunk] = proj(2 * width + c, chunk, count=PIECES_AFTER_CAST_DOT).astype(BF16)
    assert next(pieces, None) is None


def _att_proj_kernel(x0_ref, xn_ref, g_ref, shift0_ref, scale0_ref, shiftn_ref, scalen_ref,
                     w_ref, qg_ref, kg_ref, q_ref, k_ref, v_ref, sg_ref, h_a, h_b):
    step = pl.program_id(0)

    @pl.when(step == 0)
    def _():
        for piece in range(PROLOGUE_PIECES):
            _att_prologue(piece, x0_ref, g_ref, shift0_ref, scale0_ref, h_a)

    def run(cur, nxt):
        next_piece = functools.partial(_att_prologue, x_ref=xn_ref, g_ref=g_ref, shift_ref=shiftn_ref,
                                       scale_ref=scalen_ref, h_ref=nxt)
        _att_project(cur, w_ref, qg_ref, kg_ref, q_ref, k_ref, v_ref, sg_ref, next_piece)

    @pl.when(step % 2 == 0)
    def _():
        run(h_a, h_b)

    @pl.when(step % 2 == 1)
    def _():
        run(h_b, h_a)


def _att_proj(x2, norm_g, shift, scale, w_bf16, q_g, k_g, seq):
    n, d = x2.shape
    tm = PROJ_ROWS
    per_seq = seq // tm
    last = n // tm - 1
    width = ATT_HEADS * ATT_HEAD_DIM
    row = lambda i: (i, 0)
    nxt = lambda i: (jnp.minimum(i + 1, last), 0)
    nxt_bat = lambda i: (jnp.minimum(i + 1, last) // per_seq, 0, 0)
    out = jax.ShapeDtypeStruct((n, width), BF16)
    return pl.pallas_call(
        _att_proj_kernel,
        out_shape=(out, out, out, out),
        grid=(n // tm,),
        in_specs=[
            _resident((tm, d)),
            pl.BlockSpec((tm, d), nxt),
            _resident((1, d)),
            _resident((1, 1, d)),
            _resident((1, 1, d)),
            pl.BlockSpec((1, 1, d), nxt_bat),
            pl.BlockSpec((1, 1, d), nxt_bat),
            _resident(w_bf16.shape),
            _resident((1, ATT_HEAD_DIM)),
            _resident((1, ATT_HEAD_DIM)),
        ],
        out_specs=tuple(pl.BlockSpec((tm, width), row) for _ in range(4)),
        scratch_shapes=[pltpu.VMEM((tm, d), BF16), pltpu.VMEM((tm, d), BF16)],
        compiler_params=_params(("arbitrary",)),
        name="att_proj",
    )(x2, x2, norm_g, shift, scale, shift, scale, w_bf16, q_g, k_g)


def _attend(q, k, v_ext, bias, sg):
    s = lax.dot_general(q, k, (((1,), (1,)), ((), ())), preferred_element_type=F32) + bias
    m = jnp.max(s, axis=-1, keepdims=True)
    p = jnp.exp2(s - m).astype(BF16)
    o_ext = jnp.dot(p, v_ext, preferred_element_type=F32)
    o = o_ext[:, :ATT_HEAD_DIM] / o_ext[:, ATT_HEAD_DIM:]
    return (o * sg.astype(F32)).astype(BF16)


def _attention_kernel(q_ref, k_ref, v_ref, sg_ref, rows_ref, o_ref, bias_ref):
    step = pl.program_id(2)
    blocks = ATT_STEP_ROWS // ATT_QBLOCK
    left = ATT_KWIN - ATT_QBLOCK

    @pl.when(step == 0)
    def _():
        qi = lax.broadcasted_iota(jnp.int32, (ATT_QBLOCK, ATT_KWIN), 0)
        kj = lax.broadcasted_iota(jnp.int32, (ATT_QBLOCK, ATT_KWIN), 1)
        q_chunk = (qi + left) // CHUNK
        k_chunk = kj // CHUNK
        band = (k_chunk <= q_chunk) & (k_chunk >= q_chunk - LEFT_CHUNKS)
        for hh in range(2):
            tiled = jnp.broadcast_to(rows_ref[hh], (ATT_QBLOCK, ATT_BIAS_ROW))
            toeplitz = pltpu.roll(tiled, 0, 1, stride=1, stride_axis=0)
            bias_ref[hh] = jnp.where(band, toeplitz[:, :ATT_KWIN], NEG_INF)

    ones = jnp.ones((ATT_KWIN, ATT_HEAD_DIM), BF16)

    def run(blk, key_start, n_keys, bias_from):
        rows = pl.ds(blk * ATT_QBLOCK, ATT_QBLOCK)
        keys = pl.ds(key_start, n_keys)
        for hh in range(2):
            cols = slice(hh * ATT_HEAD_DIM, (hh + 1) * ATT_HEAD_DIM)
            v_ext = jnp.concatenate([v_ref[keys, cols], ones[:n_keys]], axis=1)
            o_ref[rows, cols] = _attend(
                q_ref[rows, cols], k_ref[keys, cols], v_ext,
                bias_ref[hh, :, bias_from:], sg_ref[rows, cols])

    @pl.when(step == 0)
    def _():
        for blk in range(blocks):
            first = max(blk * ATT_QBLOCK - left, 0)
            n_keys = (blk + 1) * ATT_QBLOCK - first
            run(blk, first, n_keys, ATT_KWIN - n_keys)

    @pl.when(step > 0)
    def _():
        base = step * ATT_STEP_ROWS - left
        for blk in range(blocks):
            run(blk, pl.multiple_of(base + blk * ATT_QBLOCK, ATT_QBLOCK), ATT_KWIN, 0)


def _bias_rows(rel_table):
    left = ATT_KWIN - ATT_QBLOCK
    t = rel_table.astype(F32) * LOG2E
    h = t.shape[0]
    far = jnp.broadcast_to(t[:, -1:], (h, left - MAX_REL))
    near = jnp.broadcast_to(t[:, :1], (h, ATT_KWIN - (left - MAX_REL) - t.shape[1]))
    wrap = jnp.broadcast_to(t[:, -1:], (h, ATT_BIAS_ROW - ATT_KWIN))
    return jnp.concatenate([far, t[:, ::-1], near, wrap], axis=1).reshape(h, 1, ATT_BIAS_ROW)


def _attention(q, k, v, sg, rel_table, batch, seq):
    n, width = q.shape
    rows = ATT_STEP_ROWS
    assert seq % rows == 0
    per_seq = seq // rows
    pair = 2 * ATT_HEAD_DIM
    row_pair = lambda b, h, j: (b * per_seq + j, h)
    seq_pair = lambda b, h, j: (b, h)
    return pl.pallas_call(
        _attention_kernel,
        out_shape=jax.ShapeDtypeStruct((n, width), BF16),
        grid=(batch, ATT_HEADS // 2, per_seq),
        in_specs=[
            pl.BlockSpec((rows, pair), row_pair),
            pl.BlockSpec((seq, pair), seq_pair),
            pl.BlockSpec((seq, pair), seq_pair),
            pl.BlockSpec((rows, pair), row_pair),
            pl.BlockSpec((2, 1, ATT_BIAS_ROW), lambda b, h, j: (h, 0, 0)),
        ],
        out_specs=pl.BlockSpec((rows, pair), row_pair),
        scratch_shapes=[pltpu.VMEM((2, ATT_QBLOCK, ATT_KWIN), F32)],
        compiler_params=_params(("arbitrary", "arbitrary", "arbitrary")),
        name="attention",
    )(q, k, v, sg, _bias_rows(rel_table))


def kernel(x, c, positions, norm_g, ada_w, ada_b, ret_w_in, ret_gn_g, ret_w_out,
           att_w_in, att_q_g, att_k_g, att_rel_bias, att_w_out):
    batch, seq, d = x.shape
    n = batch * seq
    mod = _adaln(c, ada_w, ada_b)
    shift = mod[:, :, :d].reshape(-1, batch, 1, d)
    scale = mod[:, :, d:2 * d].reshape(-1, batch, 1, d)
    gate = mod[:, :, 2 * d:].reshape(-1, batch, 1, d)

    x2 = x.reshape(n, d)
    pos2 = positions.reshape(n, 1)
    freq = (1.0 / (ROPE_BASE ** (jnp.arange(0, RET_QK_DIM, 2, dtype=F32) / RET_QK_DIM))).reshape(1, -1)

    q, k, v, sg = _ret_proj(x2, pos2, freq, norm_g[0].reshape(1, d), shift[0], scale[0],
                            ret_w_in[0].astype(BF16), seq)
    o = _retention(q, k, v, sg, ret_gn_g[0], batch, seq)
    x2 = _out_proj(o, ret_w_out[0].astype(BF16), x2, gate[0], seq)

    q, k, v, sg = _att_proj(x2, norm_g[1].reshape(1, d), shift[1], scale[1],
                            att_w_in[0].astype(BF16), att_q_g[0].reshape(1, -1),
                            att_k_g[0].reshape(1, -1), seq)
    o = _attention(q, k, v, sg, att_rel_bias[0], batch, seq)
    x2 = _out_proj(o, att_w_out[0].astype(BF16), x2, gate[1], seq)
    return x2.reshape(batch, seq, d)
```

```python
import functools
import itertools
import math

import jax
import jax.numpy as jnp
import numpy as np
from jax import lax
from jax.experimental import pallas as pl
from jax.experimental.pallas import tpu as pltpu

EPS = 1e-6
CHUNK = 64

RET_HEADS = 4
RET_QK_DIM = 256
RET_V_DIM = 512
ROPE_BASE = 10000.0

ATT_HEADS = 16
ATT_HEAD_DIM = 128
LEFT_CHUNKS = 8
MAX_REL = 2 * CHUNK
NEG_INF = -1e30

PROJ_ROWS = 512
OUT_PROJ_ROWS = 1024
RET_BLOCK = 256
RET_STEP_ROWS = 4096
RET_UNROLL = 4
ATT_QBLOCK = 256
ATT_UNROLL = 5
ATT_KWIN = ATT_QBLOCK + LEFT_CHUNKS * CHUNK
ATT_BIAS_ROW = 1024
LOG2E = math.log2(math.e)

VMEM_LIMIT_BYTES = 56 * 1024 * 1024

BF16 = jnp.bfloat16
F32 = jnp.float32


def _resident(shape):
    return pl.BlockSpec(shape, lambda *_: (0,) * len(shape), pipeline_mode=pl.Buffered(1))


def _params(semantics):
    return pltpu.CompilerParams(dimension_semantics=semantics, vmem_limit_bytes=VMEM_LIMIT_BYTES)


def _adaln_kernel(c_ref, w_ref, b_ref, o_ref):
    c = c_ref[...]
    cond = c * (1.0 / (1.0 + jnp.exp(-c)))
    o_ref[0] = jnp.dot(cond, w_ref[0], preferred_element_type=F32) + b_ref[0]


def _adaln(c, ada_w, ada_b):
    depth, d, d3 = ada_w.shape
    b = c.shape[0]
    col = 1024
    return pl.pallas_call(
        _adaln_kernel,
        out_shape=jax.ShapeDtypeStruct((depth, b, d3), F32),
        grid=(depth, d3 // col),
        in_specs=[
            pl.BlockSpec((b, d), lambda i, j: (0, 0)),
            pl.BlockSpec((1, d, col), lambda i, j: (i, 0, j)),
            pl.BlockSpec((1, 1, col), lambda i, j: (i, 0, j)),
        ],
        out_specs=pl.BlockSpec((1, b, col), lambda i, j: (i, 0, j)),
        compiler_params=_params(("arbitrary", "arbitrary")),
        name="adaln",
    )(c, ada_w, ada_b.reshape(depth, 1, d3))


PIECES_AFTER_BUSY_DOT = 1
PIECES_AFTER_CAST_DOT = 5
PROLOGUE_PIECES = 12 * PIECES_AFTER_BUSY_DOT + 4 * PIECES_AFTER_CAST_DOT


def _modulated_norm(x, g_ref, shift_ref, scale_ref):
    ms = jnp.mean(x * x, axis=-1, keepdims=True)
    h = x * lax.rsqrt(ms + EPS) * g_ref[...]
    h = h * (1.0 + scale_ref[0]) + shift_ref[0]
    return h.astype(BF16)


def _piece_rows(piece, tile_rows):
    n = tile_rows // PROLOGUE_PIECES
    return slice(piece * n, (piece + 1) * n)


def _silu(t):
    return t * (1.0 / (1.0 + jnp.exp(-t)))


def _zero_from(*tiles):
    acc = None
    for t in tiles:
        bits = pltpu.bitcast(t, jnp.uint32)
        for r in range(0, bits.shape[0], 8):
            for c in range(0, bits.shape[1], 128):
                part = bits[r:r + 8, c:c + 128]
                acc = part if acc is None else acc | part
    return ((acc >> 16) >> 16).astype(F32)


def _dot_then_pieces(h_ref, w_ref, col, width, pieces, count, next_piece):
    acc = jnp.dot(h_ref[...], w_ref[:, col:col + width], preferred_element_type=F32)
    for piece in itertools.islice(pieces, count):
        top = jnp.concatenate([acc[:8, :128] + next_piece(piece), acc[:8, 128:]], axis=1)
        acc = jnp.concatenate([top, acc[8:]], axis=0)
    return acc


def _ret_prologue(piece, x_ref, pos_ref, freq_ref, g_ref, shift_ref, scale_ref, h_ref, cos_ref, sin_ref):
    rows = _piece_rows(piece, x_ref.shape[0])
    h = _modulated_norm(x_ref[rows, :], g_ref, shift_ref, scale_ref)
    h_ref[rows, :] = h
    ang = pos_ref[rows, :].astype(F32) * freq_ref[...]
    cos = jnp.cos(ang)
    sin = jnp.sin(ang)
    cos_ref[rows, :] = cos
    sin_ref[rows, :] = sin
    return _zero_from(h, cos, sin)


def _ret_project(h_ref, cos_ref, sin_ref, w_ref, q_ref, k_ref, v_ref, sg_ref, next_piece):
    cos = cos_ref[...]
    sin = sin_ref[...]
    half = RET_QK_DIM // 2
    qk_width = RET_HEADS * RET_QK_DIM
    v_width = RET_HEADS * RET_V_DIM
    pieces = iter(range(PROLOGUE_PIECES))
    proj = functools.partial(_dot_then_pieces, h_ref, w_ref, pieces=pieces, next_piece=next_piece)

    for out_ref, base in ((q_ref, 0), (k_ref, qk_width)):
        for hh in range(RET_HEADS):
            acc = proj(base + hh * RET_QK_DIM, RET_QK_DIM, count=PIECES_AFTER_BUSY_DOT)
            t1 = acc[:, :half]
            t2 = acc[:, half:]
            out_ref[:, hh * RET_QK_DIM:hh * RET_QK_DIM + half] = (t1 * cos - t2 * sin).astype(BF16)
            out_ref[:, hh * RET_QK_DIM + half:(hh + 1) * RET_QK_DIM] = (t1 * sin + t2 * cos).astype(BF16)
    for c in range(0, v_width, RET_V_DIM):
        acc = proj(2 * qk_width + v_width + c, RET_V_DIM, count=PIECES_AFTER_BUSY_DOT)
        sg_ref[:, c:c + RET_V_DIM] = _silu(acc).astype(BF16)
    for c in range(0, v_width, RET_V_DIM):
        acc = proj(2 * qk_width + c, RET_V_DIM, count=PIECES_AFTER_CAST_DOT)
        v_ref[:, c:c + RET_V_DIM] = acc.astype(BF16)
    assert next(pieces, None) is None


def _ret_proj_kernel(x0_ref, xn_ref, pos0_ref, posn_ref, freq_ref, g_ref,
                     shift0_ref, scale0_ref, shiftn_ref, scalen_ref, w_ref,
                     q_ref, k_ref, v_ref, sg_ref,
                     h_a, cos_a, sin_a, h_b, cos_b, sin_b):
    step = pl.program_id(0)

    @pl.when(step == 0)
    def _():
        for piece in range(PROLOGUE_PIECES):
            _ret_prologue(piece, x0_ref, pos0_ref, freq_ref, g_ref, shift0_ref, scale0_ref, h_a, cos_a, sin_a)

    def run(cur, nxt):
        next_piece = functools.partial(_ret_prologue, x_ref=xn_ref, pos_ref=posn_ref, freq_ref=freq_ref,
                                       g_ref=g_ref, shift_ref=shiftn_ref, scale_ref=scalen_ref,
                                       h_ref=nxt[0], cos_ref=nxt[1], sin_ref=nxt[2])
        _ret_project(*cur, w_ref, q_ref, k_ref, v_ref, sg_ref, next_piece)

    set_a = (h_a, cos_a, sin_a)
    set_b = (h_b, cos_b, sin_b)

    @pl.when(step % 2 == 0)
    def _():
        run(set_a, set_b)

    @pl.when(step % 2 == 1)
    def _():
        run(set_b, set_a)


def _ret_proj(x2, pos2, freq, norm_g, shift, scale, w_bf16, seq):
    n, d = x2.shape
    tm = PROJ_ROWS
    assert n % tm == 0 and seq % tm == 0
    per_seq = seq // tm
    last = n // tm - 1
    qk_width = RET_HEADS * RET_QK_DIM
    v_width = RET_HEADS * RET_V_DIM
    half = RET_QK_DIM // 2
    row = lambda i: (i, 0)
    nxt = lambda i: (jnp.minimum(i + 1, last), 0)
    nxt_bat = lambda i: (jnp.minimum(i + 1, last) // per_seq, 0, 0)
    return pl.pallas_call(
        _ret_proj_kernel,
        out_shape=(
            jax.ShapeDtypeStruct((n, qk_width), BF16),
            jax.ShapeDtypeStruct((n, qk_width), BF16),
            jax.ShapeDtypeStruct((n, v_width), BF16),
            jax.ShapeDtypeStruct((n, v_width), BF16),
        ),
        grid=(n // tm,),
        in_specs=[
            _resident((tm, d)),
            pl.BlockSpec((tm, d), nxt),
            _resident((tm, 1)),
            pl.BlockSpec((tm, 1), nxt),
            _resident((1, half)),
            _resident((1, d)),
            _resident((1, 1, d)),
            _resident((1, 1, d)),
            pl.BlockSpec((1, 1, d), nxt_bat),
            pl.BlockSpec((1, 1, d), nxt_bat),
            _resident(w_bf16.shape),
        ],
        out_specs=(
            pl.BlockSpec((tm, qk_width), row),
            pl.BlockSpec((tm, qk_width), row),
            pl.BlockSpec((tm, v_width), row),
            pl.BlockSpec((tm, v_width), row),
        ),
        scratch_shapes=[pltpu.VMEM((tm, d), BF16), pltpu.VMEM((tm, half), F32), pltpu.VMEM((tm, half), F32),
                        pltpu.VMEM((tm, d), BF16), pltpu.VMEM((tm, half), F32), pltpu.VMEM((tm, half), F32)],
        compiler_params=_params(("arbitrary",)),
        name="ret_proj",
    )(x2, x2, pos2, pos2, freq, norm_g, shift, scale, shift, scale, w_bf16)


def _retention_kernel(q_ref, k_ref, v_ref, sg_ref, dmask_ref, qd_ref, kd_ref, bd_ref, gn_ref,
                      o_ref, state_ref):
    @pl.when(pl.program_id(2) == 0)
    def _():
        state_ref[...] = jnp.zeros_like(state_ref)

    dmask = dmask_ref[0]
    qd = qd_ref[0]
    kd = kd_ref[0]
    decay = bd_ref[0]
    gn = gn_ref[0]
    def body(blk, carry):
        rows = pl.ds(pl.multiple_of(blk * RET_BLOCK, RET_BLOCK), RET_BLOCK)
        q = q_ref[rows, :]
        k = k_ref[rows, :]
        v = v_ref[rows, :]
        state = state_ref[...]
        scores = lax.dot_general(q, k, (((1,), (1,)), ((), ())), preferred_element_type=F32)
        o = jnp.dot((scores * dmask).astype(BF16), v, preferred_element_type=F32)
        qs = (q.astype(F32) * qd).astype(BF16)
        o = o + jnp.dot(qs, state.astype(BF16), preferred_element_type=F32)
        ks = (k.astype(F32) * kd).astype(BF16)
        state_ref[...] = decay * state + lax.dot_general(
            ks, v, (((0,), (0,)), ((), ())), preferred_element_type=F32)
        ms = jnp.mean(o * o, axis=-1, keepdims=True)
        o = o * lax.rsqrt(ms + EPS) * gn
        o_ref[rows, :] = (o * sg_ref[rows, :].astype(F32)).astype(BF16)
        return carry

    lax.fori_loop(0, RET_STEP_ROWS // RET_BLOCK, body, 0, unroll=RET_UNROLL)


def _retention_tables():
    log_gamma = jnp.log(1.0 - 2.0 ** (-5.0 - jnp.arange(RET_HEADS, dtype=F32)))
    t = jnp.arange(RET_BLOCK, dtype=F32)
    dist = t[:, None] - t[None, :]
    chunk = jnp.arange(RET_BLOCK) // CHUNK
    visible = chunk[None, :] <= chunk[:, None]
    k_scale = RET_QK_DIM ** -0.5
    dmask = jnp.where(visible[None], jnp.exp(log_gamma[:, None, None] * jnp.abs(dist)[None]), 0.0) * k_scale
    qd = jnp.exp(log_gamma[:, None] * (t + 1.0))
    kd = jnp.exp(log_gamma[:, None] * (RET_BLOCK - 1.0 - t)) * k_scale
    qd = jnp.broadcast_to(qd[:, :, None], (RET_HEADS, RET_BLOCK, RET_QK_DIM))
    kd = jnp.broadcast_to(kd[:, :, None], (RET_HEADS, RET_BLOCK, RET_QK_DIM))
    bd = jnp.broadcast_to(jnp.exp(log_gamma * RET_BLOCK)[:, None, None], (RET_HEADS, 1, RET_V_DIM))
    return dmask.astype(F32), qd.astype(F32), kd.astype(F32), bd.astype(F32)


def _retention(q, k, v, sg, gn_g, batch, seq):
    n = q.shape[0]
    rows = RET_STEP_ROWS
    assert seq % rows == 0
    per_seq = seq // rows
    dmask, qd, kd, bd = _retention_tables()
    row_head = lambda b, h, j: (b * per_seq + j, h)
    head_tab = lambda b, h, j: (h, 0, 0)
    return pl.pallas_call(
        _retention_kernel,
        out_shape=jax.ShapeDtypeStruct((n, RET_HEADS * RET_V_DIM), BF16),
        grid=(batch, RET_HEADS, per_seq),
        in_specs=[
            pl.BlockSpec((rows, RET_QK_DIM), row_head),
            pl.BlockSpec((rows, RET_QK_DIM), row_head),
            pl.BlockSpec((rows, RET_V_DIM), row_head),
            pl.BlockSpec((rows, RET_V_DIM), row_head),
            pl.BlockSpec((1, RET_BLOCK, RET_BLOCK), head_tab),
            pl.BlockSpec((1, RET_BLOCK, RET_QK_DIM), head_tab),
            pl.BlockSpec((1, RET_BLOCK, RET_QK_DIM), head_tab),
            pl.BlockSpec((1, 1, RET_V_DIM), head_tab),
            pl.BlockSpec((1, 1, RET_V_DIM), head_tab),
        ],
        out_specs=pl.BlockSpec((rows, RET_V_DIM), row_head),
        scratch_shapes=[pltpu.VMEM((RET_QK_DIM, RET_V_DIM), F32)],
        compiler_params=_params(("arbitrary", "arbitrary", "arbitrary")),
        name="retention",
    )(q, k, v, sg, dmask, qd, kd, bd, gn_g.reshape(RET_HEADS, 1, RET_V_DIM))


def _out_proj_kernel(o_ref, w_ref, x_ref, gate_ref, y_ref):
    out = jnp.dot(o_ref[...], w_ref[...], preferred_element_type=F32)
    y_ref[...] = x_ref[...] + gate_ref[0] * out


def _out_proj(o, w_bf16, x2, gate, seq):
    n, d = x2.shape
    width = o.shape[1]
    tm = OUT_PROJ_ROWS
    assert n % tm == 0 and seq % tm == 0
    per_seq = seq // tm
    row = lambda i: (i, 0)
    return pl.pallas_call(
        _out_proj_kernel,
        out_shape=jax.ShapeDtypeStruct((n, d), F32),
        grid=(n // tm,),
        in_specs=[
            pl.BlockSpec((tm, width), row),
            _resident(w_bf16.shape),
            pl.BlockSpec((tm, d), row),
            pl.BlockSpec((1, 1, d), lambda i: (i // per_seq, 0, 0)),
        ],
        out_specs=pl.BlockSpec((tm, d), row),
        compiler_params=_params(("arbitrary",)),
        name="out_proj",
    )(o, w_bf16, x2, gate)


def _att_prologue(piece, x_ref, g_ref, shift_ref, scale_ref, h_ref):
    rows = _piece_rows(piece, x_ref.shape[0])
    h = _modulated_norm(x_ref[rows, :], g_ref, shift_ref, scale_ref)
    h_ref[rows, :] = h
    return _zero_from(h)


def _att_project(h_ref, w_ref, qg_ref, kg_ref, q_ref, k_ref, v_ref, sg_ref, next_piece):
    width = ATT_HEADS * ATT_HEAD_DIM
    chunk = 512
    pieces = iter(range(PROLOGUE_PIECES))
    proj = functools.partial(_dot_then_pieces, h_ref, w_ref, pieces=pieces, next_piece=next_piece)

    for out_ref, gain_ref, base, post in ((q_ref, qg_ref, 0, ATT_HEAD_DIM ** -0.5 * LOG2E), (k_ref, kg_ref, width, 1.0)):
        gain = gain_ref[...] * post
        for c in range(0, width, chunk):
            acc = proj(base + c, chunk, count=PIECES_AFTER_BUSY_DOT)
            for hh in range(chunk // ATT_HEAD_DIM):
                a = acc[:, hh * ATT_HEAD_DIM:(hh + 1) * ATT_HEAD_DIM]
                ms = jnp.mean(a * a, axis=-1, keepdims=True)
                out_ref[:, c + hh * ATT_HEAD_DIM:c + (hh + 1) * ATT_HEAD_DIM] = (
                    a * lax.rsqrt(ms + EPS) * gain).astype(BF16)
    for c in range(0, width, chunk):
        sg_ref[:, c:c + chunk] = _silu(proj(3 * width + c, chunk, count=PIECES_AFTER_BUSY_DOT)).astype(BF16)
    for c in range(0, width, chunk):
        v_ref[:, c:c + chunk] = proj(2 * width + c, chunk, count=PIECES_AFTER_CAST_DOT).astype(BF16)
    assert next(pieces, None) is None


def _att_proj_kernel(x0_ref, xn_ref, g_ref, shift0_ref, scale0_ref, shiftn_ref, scalen_ref,
                     w_ref, qg_ref, kg_ref, q_ref, k_ref, v_ref, sg_ref, h_a, h_b):
    step = pl.program_id(0)

    @pl.when(step == 0)
    def _():
        for piece in range(PROLOGUE_PIECES):
            _att_prologue(piece, x0_ref, g_ref, shift0_ref, scale0_ref, h_a)

    def run(cur, nxt):
        next_piece = functools.partial(_att_prologue, x_ref=xn_ref, g_ref=g_ref, shift_ref=shiftn_ref,
                                       scale_ref=scalen_ref, h_ref=nxt)
        _att_project(cur, w_ref, qg_ref, kg_ref, q_ref, k_ref, v_ref, sg_ref, next_piece)

    @pl.when(step % 2 == 0)
    def _():
        run(h_a, h_b)

    @pl.when(step % 2 == 1)
    def _():
        run(h_b, h_a)


def _att_proj(x2, norm_g, shift, scale, w_bf16, q_g, k_g, seq):
    n, d = x2.shape
    tm = PROJ_ROWS
    per_seq = seq // tm
    last = n // tm - 1
    width = ATT_HEADS * ATT_HEAD_DIM
    row = lambda i: (i, 0)
    nxt = lambda i: (jnp.minimum(i + 1, last), 0)
    nxt_bat = lambda i: (jnp.minimum(i + 1, last) // per_seq, 0, 0)
    out = jax.ShapeDtypeStruct((n, width), BF16)
    return pl.pallas_call(
        _att_proj_kernel,
        out_shape=(out, out, out, out),
        grid=(n // tm,),
        in_specs=[
            _resident((tm, d)),
            pl.BlockSpec((tm, d), nxt),
            _resident((1, d)),
            _resident((1, 1, d)),
            _resident((1, 1, d)),
            pl.BlockSpec((1, 1, d), nxt_bat),
            pl.BlockSpec((1, 1, d), nxt_bat),
            _resident(w_bf16.shape),
            _resident((1, ATT_HEAD_DIM)),
            _resident((1, ATT_HEAD_DIM)),
        ],
        out_specs=tuple(pl.BlockSpec((tm, width), row) for _ in range(4)),
        scratch_shapes=[pltpu.VMEM((tm, d), BF16), pltpu.VMEM((tm, d), BF16)],
        compiler_params=_params(("arbitrary",)),
        name="att_proj",
    )(x2, x2, norm_g, shift, scale, shift, scale, w_bf16, q_g, k_g)


def _attend(q, k, v_ext, bias, sg):
    s = lax.dot_general(q, k, (((1,), (1,)), ((), ())), preferred_element_type=F32) + bias
    m = jnp.max(s, axis=-1, keepdims=True)
    p = jnp.exp2(s - m).astype(BF16)
    o_ext = jnp.dot(p, v_ext, preferred_element_type=F32)
    o = o_ext[:, :ATT_HEAD_DIM] / o_ext[:, ATT_HEAD_DIM:]
    return (o * sg.astype(F32)).astype(BF16)


def _attention_kernel(q_ref, k_ref, v_ref, sg_ref, rows_ref, o_ref, bias_ref):
    left = ATT_KWIN - ATT_QBLOCK
    n_blocks = q_ref.shape[0] // ATT_QBLOCK
    first_full = left // ATT_QBLOCK

    qi = lax.broadcasted_iota(jnp.int32, (ATT_QBLOCK, ATT_KWIN), 0)
    kj = lax.broadcasted_iota(jnp.int32, (ATT_QBLOCK, ATT_KWIN), 1)
    q_chunk = (qi + left) // CHUNK
    k_chunk = kj // CHUNK
    band = (k_chunk <= q_chunk) & (k_chunk >= q_chunk - LEFT_CHUNKS)
    for hh in range(2):
        tiled = jnp.broadcast_to(rows_ref[hh], (ATT_QBLOCK, ATT_BIAS_ROW))
        toeplitz = pltpu.roll(tiled, 0, 1, stride=1, stride_axis=0)
        bias_ref[hh] = jnp.where(band, toeplitz[:, :ATT_KWIN], NEG_INF)

    ones = jnp.ones((ATT_KWIN, ATT_HEAD_DIM), BF16)

    def run(row_start, key_start, n_keys):
        rows = pl.ds(row_start, ATT_QBLOCK)
        keys = pl.ds(key_start, n_keys)
        for hh in range(2):
            cols = slice(hh * ATT_HEAD_DIM, (hh + 1) * ATT_HEAD_DIM)
            v_ext = jnp.concatenate([v_ref[keys, cols], ones[:n_keys]], axis=1)
            o_ref[rows, cols] = _attend(
                q_ref[rows, cols], k_ref[keys, cols], v_ext,
                bias_ref[hh, :, ATT_KWIN - n_keys:], sg_ref[rows, cols])

    for blk in range(first_full):
        run(blk * ATT_QBLOCK, 0, (blk + 1) * ATT_QBLOCK)

    def body(blk, carry):
        row_start = pl.multiple_of(blk * ATT_QBLOCK, ATT_QBLOCK)
        run(row_start, pl.multiple_of(row_start - left, ATT_QBLOCK), ATT_KWIN)
        return carry

    lax.fori_loop(first_full, n_blocks, body, 0, unroll=ATT_UNROLL)


def _bias_rows(rel_table):
    left = ATT_KWIN - ATT_QBLOCK
    t = rel_table.astype(F32) * LOG2E
    h = t.shape[0]
    far = jnp.broadcast_to(t[:, -1:], (h, left - MAX_REL))
    near = jnp.broadcast_to(t[:, :1], (h, ATT_KWIN - (left - MAX_REL) - t.shape[1]))
    wrap = jnp.broadcast_to(t[:, -1:], (h, ATT_BIAS_ROW - ATT_KWIN))
    return jnp.concatenate([far, t[:, ::-1], near, wrap], axis=1).reshape(h, 1, ATT_BIAS_ROW)


def _attention(q, k, v, sg, rel_table, batch, seq):
    n, width = q.shape
    assert seq % ATT_QBLOCK == 0
    pair = 2 * ATT_HEAD_DIM
    seq_pair = pl.BlockSpec((seq, pair), lambda b, h: (b, h))
    return pl.pallas_call(
        _attention_kernel,
        out_shape=jax.ShapeDtypeStruct((n, width), BF16),
        grid=(batch, ATT_HEADS // 2),
        in_specs=[seq_pair, seq_pair, seq_pair, seq_pair,
                  pl.BlockSpec((2, 1, ATT_BIAS_ROW), lambda b, h: (h, 0, 0))],
        out_specs=seq_pair,
        scratch_shapes=[pltpu.VMEM((2, ATT_QBLOCK, ATT_KWIN), F32)],
        compiler_params=_params(("arbitrary", "arbitrary")),
        name="attention",
    )(q, k, v, sg, _bias_rows(rel_table))


def kernel(x, c, positions, norm_g, ada_w, ada_b, ret_w_in, ret_gn_g, ret_w_out,
           att_w_in, att_q_g, att_k_g, att_rel_bias, att_w_out):
    batch, seq, d = x.shape
    n = batch * seq
    mod = _adaln(c, ada_w, ada_b)
    shift = mod[:, :, :d].reshape(-1, batch, 1, d)
    scale = mod[:, :, d:2 * d].reshape(-1, batch, 1, d)
    gate = mod[:, :, 2 * d:].reshape(-1, batch, 1, d)

    x2 = x.reshape(n, d)
    pos2 = positions.reshape(n, 1)
    freq = (1.0 / (ROPE_BASE ** (jnp.arange(0, RET_QK_DIM, 2, dtype=F32) / RET_QK_DIM))).reshape(1, -1)

    q, k, v, sg = _ret_proj(x2, pos2, freq, norm_g[0].reshape(1, d), shift[0], scale[0],
                            ret_w_in[0].astype(BF16), seq)
    o = _retention(q, k, v, sg, ret_gn_g[0], batch, seq)
    x2 = _out_proj(o, ret_w_out[0].astype(BF16), x2, gate[0], seq)

    q, k, v, sg = _att_proj(x2, norm_g[1].reshape(1, d), shift[1], scale[1],
                            att_w_in[0].astype(BF16), att_q_g[0].reshape(1, -1),
                            att_k_g[0].reshape(1, -1), seq)
    o = _attention(q, k, v, sg, att_rel_bias[0], batch, seq)
    x2 = _out_proj(o, att_w_out[0].astype(BF16), x2, gate[1], seq)
    return x2.reshape(batch, seq, d)
```

```python
import functools
import itertools
import math

import jax
import jax.numpy as jnp
import numpy as np
from jax import lax
from jax.experimental import pallas as pl
from jax.experimental.pallas import tpu as pltpu

EPS = 1e-6
CHUNK = 64

RET_HEADS = 4
RET_QK_DIM = 256
RET_V_DIM = 512
ROPE_BASE = 10000.0

ATT_HEADS = 16
ATT_HEAD_DIM = 128
LEFT_CHUNKS = 8
MAX_REL = 2 * CHUNK
NEG_INF = -1e30

PROJ_ROWS = 512
OUT_PROJ_ROWS = 1024
RET_BLOCK = 256
RET_STEP_ROWS = 4096
RET_UNROLL = 4
ATT_QBLOCK = 256
ATT_UNROLL = 5
ATT_KWIN = ATT_QBLOCK + LEFT_CHUNKS * CHUNK
ATT_BIAS_ROW = 1024
LOG2E = math.log2(math.e)

VMEM_LIMIT_BYTES = 56 * 1024 * 1024

BF16 = jnp.bfloat16
F32 = jnp.float32


def _resident(shape):
    return pl.BlockSpec(shape, lambda *_: (0,) * len(shape), pipeline_mode=pl.Buffered(1))


def _params(semantics):
    return pltpu.CompilerParams(dimension_semantics=semantics, vmem_limit_bytes=VMEM_LIMIT_BYTES)


def _adaln_kernel(c_ref, w_ref, b_ref, o_ref):
    c = c_ref[...]
    cond = c * (1.0 / (1.0 + jnp.exp(-c)))
    o_ref[0] = jnp.dot(cond, w_ref[0], preferred_element_type=F32) + b_ref[0]


def _adaln(c, ada_w, ada_b):
    depth, d, d3 = ada_w.shape
    b = c.shape[0]
    col = 1024
    return pl.pallas_call(
        _adaln_kernel,
        out_shape=jax.ShapeDtypeStruct((depth, b, d3), F32),
        grid=(depth, d3 // col),
        in_specs=[
            pl.BlockSpec((b, d), lambda i, j: (0, 0)),
            pl.BlockSpec((1, d, col), lambda i, j: (i, 0, j)),
            pl.BlockSpec((1, 1, col), lambda i, j: (i, 0, j)),
        ],
        out_specs=pl.BlockSpec((1, b, col), lambda i, j: (i, 0, j)),
        compiler_params=_params(("arbitrary", "arbitrary")),
        name="adaln",
    )(c, ada_w, ada_b.reshape(depth, 1, d3))


PIECES_AFTER_BUSY_DOT = 1
PIECES_AFTER_CAST_DOT = 5
PROLOGUE_PIECES = 12 * PIECES_AFTER_BUSY_DOT + 4 * PIECES_AFTER_CAST_DOT


def _modulated_norm(x, g_ref, shift_ref, scale_ref):
    ms = jnp.mean(x * x, axis=-1, keepdims=True)
    h = x * lax.rsqrt(ms + EPS) * g_ref[...]
    h = h * (1.0 + scale_ref[0]) + shift_ref[0]
    return h.astype(BF16)


def _piece_rows(piece, tile_rows):
    n = tile_rows // PROLOGUE_PIECES
    return slice(piece * n, (piece + 1) * n)


def _silu(t):
    return t * (1.0 / (1.0 + jnp.exp(-t)))


def _zero_from(*tiles):
    acc = None
    for t in tiles:
        bits = pltpu.bitcast(t, jnp.uint32)
        for r in range(0, bits.shape[0], 8):
            for c in range(0, bits.shape[1], 128):
                part = bits[r:r + 8, c:c + 128]
                acc = part if acc is None else acc | part
    return ((acc >> 16) >> 16).astype(F32)


def _dot_then_pieces(h_ref, w_ref, col, width, pieces, count, next_piece):
    acc = jnp.dot(h_ref[...], w_ref[:, col:col + width], preferred_element_type=F32)
    for piece in itertools.islice(pieces, count):
        top = jnp.concatenate([acc[:8, :128] + next_piece(piece), acc[:8, 128:]], axis=1)
        acc = jnp.concatenate([top, acc[8:]], axis=0)
    return acc


HALF_PI_PARTS = (1.578125, -0.00732421875, -4.470348358154297e-06, 1.5893254712295857e-08)
SMALL_ANGLE_LIMIT = 2 ** 17


def _sincos_small(x):
    k = jnp.floor(x * (2.0 / math.pi) + 0.5)
    r = x
    for part in HALF_PI_PARTS:
        r = r - k * part
    z = r * r
    s = r + r * z * (-1.6666654611e-1 + z * (8.3321608736e-3 + z * -1.9515295891e-4))
    c = 1.0 - 0.5 * z + z * z * (4.166664568298827e-2 + z * (-1.388731625493765e-3 + z * 2.443315711809948e-5))
    q = k.astype(jnp.int32)
    swap = (q & 1) != 0
    cos = jnp.where(swap, s, c)
    sin = jnp.where(swap, c, s)
    cos = jnp.where(((q + 1) & 2) != 0, -cos, cos)
    sin = jnp.where((q & 2) != 0, -sin, sin)
    return cos, sin


def _ret_prologue(piece, x_ref, pos_ref, freq_ref, g_ref, shift_ref, scale_ref, h_ref, cos_ref, sin_ref,
                  small_angles):
    rows = _piece_rows(piece, x_ref.shape[0])
    h = _modulated_norm(x_ref[rows, :], g_ref, shift_ref, scale_ref)
    h_ref[rows, :] = h
    ang = pos_ref[rows, :].astype(F32) * freq_ref[...]
    if small_angles:
        cos, sin = _sincos_small(ang)
    else:
        cos = jnp.cos(ang)
        sin = jnp.sin(ang)
    cos_ref[rows, :] = cos
    sin_ref[rows, :] = sin
    return _zero_from(h, cos, sin)


def _ret_project(h_ref, cos_ref, sin_ref, w_ref, q_ref, k_ref, v_ref, sg_ref, next_piece):
    cos = cos_ref[...]
    sin = sin_ref[...]
    half = RET_QK_DIM // 2
    qk_width = RET_HEADS * RET_QK_DIM
    v_width = RET_HEADS * RET_V_DIM
    pieces = iter(range(PROLOGUE_PIECES))
    proj = functools.partial(_dot_then_pieces, h_ref, w_ref, pieces=pieces, next_piece=next_piece)

    for out_ref, base in ((q_ref, 0), (k_ref, qk_width)):
        for hh in range(RET_HEADS):
            acc = proj(base + hh * RET_QK_DIM, RET_QK_DIM, count=PIECES_AFTER_BUSY_DOT)
            t1 = acc[:, :half]
            t2 = acc[:, half:]
            out_ref[:, hh * RET_QK_DIM:hh * RET_QK_DIM + half] = (t1 * cos - t2 * sin).astype(BF16)
            out_ref[:, hh * RET_QK_DIM + half:(hh + 1) * RET_QK_DIM] = (t1 * sin + t2 * cos).astype(BF16)
    for c in range(0, v_width, RET_V_DIM):
        acc = proj(2 * qk_width + v_width + c, RET_V_DIM, count=PIECES_AFTER_BUSY_DOT)
        sg_ref[:, c:c + RET_V_DIM] = _silu(acc).astype(BF16)
    for c in range(0, v_width, RET_V_DIM):
        acc = proj(2 * qk_width + c, RET_V_DIM, count=PIECES_AFTER_CAST_DOT)
        v_ref[:, c:c + RET_V_DIM] = acc.astype(BF16)
    assert next(pieces, None) is None


def _ret_proj_kernel(x0_ref, xn_ref, pos0_ref, posn_ref, freq_ref, g_ref,
                     shift0_ref, scale0_ref, shiftn_ref, scalen_ref, w_ref,
                     q_ref, k_ref, v_ref, sg_ref,
                     h_a, cos_a, sin_a, h_b, cos_b, sin_b, *, small_angles):
    step = pl.program_id(0)

    @pl.when(step == 0)
    def _():
        for piece in range(PROLOGUE_PIECES):
            _ret_prologue(piece, x0_ref, pos0_ref, freq_ref, g_ref, shift0_ref, scale0_ref, h_a, cos_a, sin_a,
                          small_angles)

    def run(cur, nxt):
        next_piece = functools.partial(_ret_prologue, x_ref=xn_ref, pos_ref=posn_ref, freq_ref=freq_ref,
                                       g_ref=g_ref, shift_ref=shiftn_ref, scale_ref=scalen_ref,
                                       h_ref=nxt[0], cos_ref=nxt[1], sin_ref=nxt[2], small_angles=small_angles)
        _ret_project(*cur, w_ref, q_ref, k_ref, v_ref, sg_ref, next_piece)

    set_a = (h_a, cos_a, sin_a)
    set_b = (h_b, cos_b, sin_b)

    @pl.when(step % 2 == 0)
    def _():
        run(set_a, set_b)

    @pl.when(step % 2 == 1)
    def _():
        run(set_b, set_a)


def _ret_proj(x2, pos2, freq, norm_g, shift, scale, w_bf16, seq, small_angles):
    n, d = x2.shape
    tm = PROJ_ROWS
    assert n % tm == 0 and seq % tm == 0
    per_seq = seq // tm
    last = n // tm - 1
    qk_width = RET_HEADS * RET_QK_DIM
    v_width = RET_HEADS * RET_V_DIM
    half = RET_QK_DIM // 2
    row = lambda i: (i, 0)
    nxt = lambda i: (jnp.minimum(i + 1, last), 0)
    nxt_bat = lambda i: (jnp.minimum(i + 1, last) // per_seq, 0, 0)
    return pl.pallas_call(
        functools.partial(_ret_proj_kernel, small_angles=small_angles),
        out_shape=(
            jax.ShapeDtypeStruct((n, qk_width), BF16),
            jax.ShapeDtypeStruct((n, qk_width), BF16),
            jax.ShapeDtypeStruct((n, v_width), BF16),
            jax.ShapeDtypeStruct((n, v_width), BF16),
        ),
        grid=(n // tm,),
        in_specs=[
            _resident((tm, d)),
            pl.BlockSpec((tm, d), nxt),
            _resident((tm, 1)),
            pl.BlockSpec((tm, 1), nxt),
            _resident((1, half)),
            _resident((1, d)),
            _resident((1, 1, d)),
            _resident((1, 1, d)),
            pl.BlockSpec((1, 1, d), nxt_bat),
            pl.BlockSpec((1, 1, d), nxt_bat),
            _resident(w_bf16.shape),
        ],
        out_specs=(
            pl.BlockSpec((tm, qk_width), row),
            pl.BlockSpec((tm, qk_width), row),
            pl.BlockSpec((tm, v_width), row),
            pl.BlockSpec((tm, v_width), row),
        ),
        scratch_shapes=[pltpu.VMEM((tm, d), BF16), pltpu.VMEM((tm, half), F32), pltpu.VMEM((tm, half), F32),
                        pltpu.VMEM((tm, d), BF16), pltpu.VMEM((tm, half), F32), pltpu.VMEM((tm, half), F32)],
        compiler_params=_params(("arbitrary",)),
        name="ret_proj",
    )(x2, x2, pos2, pos2, freq, norm_g, shift, scale, shift, scale, w_bf16)


def _retention_kernel(q_ref, k_ref, v_ref, sg_ref, dmask_ref, qd_ref, kd_ref, bd_ref, gn_ref,
                      o_ref, state_ref):
    @pl.when(pl.program_id(2) == 0)
    def _():
        state_ref[...] = jnp.zeros_like(state_ref)

    dmask = dmask_ref[0]
    qd = qd_ref[0]
    kd = kd_ref[0]
    decay = bd_ref[0]
    gn = gn_ref[0]
    def body(blk, carry):
        rows = pl.ds(pl.multiple_of(blk * RET_BLOCK, RET_BLOCK), RET_BLOCK)
        q = q_ref[rows, :]
        k = k_ref[rows, :]
        v = v_ref[rows, :]
        state = state_ref[...]
        scores = lax.dot_general(q, k, (((1,), (1,)), ((), ())), preferred_element_type=F32)
        o = jnp.dot((scores * dmask).astype(BF16), v, preferred_element_type=F32)
        qs = (q.astype(F32) * qd).astype(BF16)
        o = o + jnp.dot(qs, state.astype(BF16), preferred_element_type=F32)
        ks = (k.astype(F32) * kd).astype(BF16)
        state_ref[...] = decay * state + lax.dot_general(
            ks, v, (((0,), (0,)), ((), ())), preferred_element_type=F32)
        ms = jnp.mean(o * o, axis=-1, keepdims=True)
        o = o * lax.rsqrt(ms + EPS) * gn
        o_ref[rows, :] = (o * sg_ref[rows, :].astype(F32)).astype(BF16)
        return carry

    lax.fori_loop(0, RET_STEP_ROWS // RET_BLOCK, body, 0, unroll=RET_UNROLL)


def _retention_tables():
    log_gamma = jnp.log(1.0 - 2.0 ** (-5.0 - jnp.arange(RET_HEADS, dtype=F32)))
    t = jnp.arange(RET_BLOCK, dtype=F32)
    dist = t[:, None] - t[None, :]
    chunk = jnp.arange(RET_BLOCK) // CHUNK
    visible = chunk[None, :] <= chunk[:, None]
    k_scale = RET_QK_DIM ** -0.5
    dmask = jnp.where(visible[None], jnp.exp(log_gamma[:, None, None] * jnp.abs(dist)[None]), 0.0) * k_scale
    qd = jnp.exp(log_gamma[:, None] * (t + 1.0))
    kd = jnp.exp(log_gamma[:, None] * (RET_BLOCK - 1.0 - t)) * k_scale
    qd = jnp.broadcast_to(qd[:, :, None], (RET_HEADS, RET_BLOCK, RET_QK_DIM))
    kd = jnp.broadcast_to(kd[:, :, None], (RET_HEADS, RET_BLOCK, RET_QK_DIM))
    bd = jnp.broadcast_to(jnp.exp(log_gamma * RET_BLOCK)[:, None, None], (RET_HEADS, 1, RET_V_DIM))
    return dmask.astype(F32), qd.astype(F32), kd.astype(F32), bd.astype(F32)


def _retention(q, k, v, sg, gn_g, batch, seq):
    n = q.shape[0]
    rows = RET_STEP_ROWS
    assert seq % rows == 0
    per_seq = seq // rows
    dmask, qd, kd, bd = _retention_tables()
    row_head = lambda b, h, j: (b * per_seq + j, h)
    head_tab = lambda b, h, j: (h, 0, 0)
    return pl.pallas_call(
        _retention_kernel,
        out_shape=jax.ShapeDtypeStruct((n, RET_HEADS * RET_V_DIM), BF16),
        grid=(batch, RET_HEADS, per_seq),
        in_specs=[
            pl.BlockSpec((rows, RET_QK_DIM), row_head),
            pl.BlockSpec((rows, RET_QK_DIM), row_head),
            pl.BlockSpec((rows, RET_V_DIM), row_head),
            pl.BlockSpec((rows, RET_V_DIM), row_head),
            pl.BlockSpec((1, RET_BLOCK, RET_BLOCK), head_tab),
            pl.BlockSpec((1, RET_BLOCK, RET_QK_DIM), head_tab),
            pl.BlockSpec((1, RET_BLOCK, RET_QK_DIM), head_tab),
            pl.BlockSpec((1, 1, RET_V_DIM), head_tab),
            pl.BlockSpec((1, 1, RET_V_DIM), head_tab),
        ],
        out_specs=pl.BlockSpec((rows, RET_V_DIM), row_head),
        scratch_shapes=[pltpu.VMEM((RET_QK_DIM, RET_V_DIM), F32)],
        compiler_params=_params(("arbitrary", "arbitrary", "arbitrary")),
        name="retention",
    )(q, k, v, sg, dmask, qd, kd, bd, gn_g.reshape(RET_HEADS, 1, RET_V_DIM))


def _ret_unit(blk, head, q_s, k_s, v_s, sg_s, dmask_ref, qd_ref, kd_ref, bd_ref, gn_ref,
              state_ref, o_ref, reset):
    rows = slice(blk * RET_BLOCK, (blk + 1) * RET_BLOCK)
    qk_cols = slice(head * RET_QK_DIM, (head + 1) * RET_QK_DIM)
    v_cols = slice(head * RET_V_DIM, (head + 1) * RET_V_DIM)
    q = q_s[rows, qk_cols]
    k = k_s[rows, qk_cols]
    v = v_s[rows, v_cols]
    state = state_ref[head]
    if blk == 0:
        state = jnp.where(reset, 0.0, state)
    scores = lax.dot_general(q, k, (((1,), (1,)), ((), ())), preferred_element_type=F32)
    o = jnp.dot((scores * dmask_ref[head]).astype(BF16), v, preferred_element_type=F32)
    qs = (q.astype(F32) * qd_ref[head]).astype(BF16)
    o = o + jnp.dot(qs, state.astype(BF16), preferred_element_type=F32)
    ks = (k.astype(F32) * kd_ref[head]).astype(BF16)
    state_ref[head] = bd_ref[head] * state + lax.dot_general(
        ks, v, (((0,), (0,)), ((), ())), preferred_element_type=F32)
    ms = jnp.mean(o * o, axis=-1, keepdims=True)
    o = o * lax.rsqrt(ms + EPS) * gn_ref[head]
    out = (o * sg_s[rows, v_cols].astype(F32)).astype(BF16)
    o_ref[rows, v_cols] = out
    return _zero_from(*(out[r:r + 16] for r in range(0, RET_BLOCK, 64)))


def _dot_then(h_ref, w_ref, col, width, extras):
    acc = jnp.dot(h_ref[...], w_ref[:, col:col + width], preferred_element_type=F32)
    for item in extras:
        top = jnp.concatenate([acc[:8, :128] + item(), acc[:8, 128:]], axis=1)
        acc = jnp.concatenate([top, acc[8:]], axis=0)
    return acc


def _ret_layer_step(cur, nxt, prev, w_ref, next_piece, ret_unit):
    h_ref, cos_ref, sin_ref, q_s, k_s, v_s, sg_s = cur
    cos = cos_ref[...]
    sin = sin_ref[...]
    half = RET_QK_DIM // 2
    qk_width = RET_HEADS * RET_QK_DIM
    v_width = RET_HEADS * RET_V_DIM
    blocks = h_ref.shape[0] // RET_BLOCK
    pieces = iter([functools.partial(next_piece, p) for p in range(PROLOGUE_PIECES)])
    units = iter([functools.partial(ret_unit, blk, head)
                  for blk in range(blocks) for head in range(RET_HEADS)])

    def extras(n_pieces, n_units):
        return list(itertools.islice(units, n_units)) + list(itertools.islice(pieces, n_pieces))

    for out_ref, base in ((q_s, 0), (k_s, qk_width)):
        for hh in range(RET_HEADS):
            acc = _dot_then(h_ref, w_ref, base + hh * RET_QK_DIM, RET_QK_DIM, extras(2, 0))
            t1 = acc[:, :half]
            t2 = acc[:, half:]
            out_ref[:, hh * RET_QK_DIM:hh * RET_QK_DIM + half] = (t1 * cos - t2 * sin).astype(BF16)
            out_ref[:, hh * RET_QK_DIM + half:(hh + 1) * RET_QK_DIM] = (t1 * sin + t2 * cos).astype(BF16)
    for c in range(0, v_width, RET_V_DIM):
        acc = _dot_then(h_ref, w_ref, 2 * qk_width + v_width + c, RET_V_DIM, extras(1, 1))
        sg_s[:, c:c + RET_V_DIM] = _silu(acc).astype(BF16)
    for c in range(0, v_width, RET_V_DIM):
        acc = _dot_then(h_ref, w_ref, 2 * qk_width + c, RET_V_DIM, extras(3, 1))
        v_s[:, c:c + RET_V_DIM] = acc.astype(BF16)
    assert next(pieces, None) is None and next(units, None) is None


def _ret_layer_kernel(x0_ref, xn_ref, pos0_ref, posn_ref, freq_ref, g_ref,
                      shift0_ref, scale0_ref, shiftn_ref, scalen_ref, w_ref,
                      dmask_ref, qd_ref, kd_ref, bd_ref, gn_ref, o_ref,
                      h_a, cos_a, sin_a, q_a, k_a, v_a, sg_a,
                      h_b, cos_b, sin_b, q_b, k_b, v_b, sg_b, state_ref, *, tiles_per_seq, small_angles):
    step = pl.program_id(0)
    set_a = (h_a, cos_a, sin_a, q_a, k_a, v_a, sg_a)
    set_b = (h_b, cos_b, sin_b, q_b, k_b, v_b, sg_b)

    @pl.when(step == 0)
    def _():
        for piece in range(PROLOGUE_PIECES):
            _ret_prologue(piece, x0_ref, pos0_ref, freq_ref, g_ref, shift0_ref, scale0_ref, h_a, cos_a, sin_a,
                          small_angles)
        for ref in (q_b, k_b, v_b, sg_b, state_ref):
            ref[...] = jnp.zeros_like(ref)

    reset = step % tiles_per_seq == 1

    def run(cur, prev):
        next_piece = functools.partial(_ret_prologue, x_ref=xn_ref, pos_ref=posn_ref, freq_ref=freq_ref,
                                       g_ref=g_ref, shift_ref=shiftn_ref, scale_ref=scalen_ref,
                                       h_ref=prev[0], cos_ref=prev[1], sin_ref=prev[2], small_angles=small_angles)
        ret_unit = functools.partial(_ret_unit, q_s=prev[3], k_s=prev[4], v_s=prev[5], sg_s=prev[6],
                                     dmask_ref=dmask_ref, qd_ref=qd_ref, kd_ref=kd_ref, bd_ref=bd_ref,
                                     gn_ref=gn_ref, state_ref=state_ref, o_ref=o_ref, reset=reset)
        _ret_layer_step(cur, None, prev, w_ref, next_piece, ret_unit)

    @pl.when(step % 2 == 0)
    def _():
        run(set_a, set_b)

    @pl.when(step % 2 == 1)
    def _():
        run(set_b, set_a)


def _ret_layer(x2, pos2, freq, norm_g, shift, scale, w_bf16, gn_g, seq, small_angles):
    n, d = x2.shape
    tm = PROJ_ROWS
    assert n % tm == 0 and seq % tm == 0 and tm % RET_BLOCK == 0
    per_seq = seq // tm
    n_tiles = n // tm
    last = n_tiles - 1
    qk_width = RET_HEADS * RET_QK_DIM
    v_width = RET_HEADS * RET_V_DIM
    half = RET_QK_DIM // 2
    dmask, qd, kd, bd = _retention_tables()
    nxt = lambda i: (jnp.minimum(i + 1, last), 0)
    nxt_bat = lambda i: (jnp.minimum(i + 1, last) // per_seq, 0, 0)
    tile_set = [pltpu.VMEM((tm, d), BF16), pltpu.VMEM((tm, half), F32), pltpu.VMEM((tm, half), F32),
                pltpu.VMEM((tm, qk_width), BF16), pltpu.VMEM((tm, qk_width), BF16),
                pltpu.VMEM((tm, v_width), BF16), pltpu.VMEM((tm, v_width), BF16)]
    return pl.pallas_call(
        functools.partial(_ret_layer_kernel, tiles_per_seq=per_seq, small_angles=small_angles),
        out_shape=jax.ShapeDtypeStruct((n, v_width), BF16),
        grid=(n_tiles + 1,),
        in_specs=[
            _resident((tm, d)),
            pl.BlockSpec((tm, d), nxt),
            _resident((tm, 1)),
            pl.BlockSpec((tm, 1), nxt),
            _resident((1, half)),
            _resident((1, d)),
            _resident((1, 1, d)),
            _resident((1, 1, d)),
            pl.BlockSpec((1, 1, d), nxt_bat),
            pl.BlockSpec((1, 1, d), nxt_bat),
            _resident(w_bf16.shape),
            _resident(dmask.shape),
            _resident(qd.shape),
            _resident(kd.shape),
            _resident(bd.shape),
            _resident((RET_HEADS, 1, RET_V_DIM)),
        ],
        out_specs=pl.BlockSpec((tm, v_width), lambda i: (jnp.maximum(i - 1, 0), 0)),
        scratch_shapes=tile_set + tile_set + [pltpu.VMEM((RET_HEADS, RET_QK_DIM, RET_V_DIM), F32)],
        compiler_params=_params(("arbitrary",)),
        name="ret_layer",
    )(x2, x2, pos2, pos2, freq, norm_g, shift, scale, shift, scale, w_bf16,
      dmask, qd, kd, bd, gn_g.reshape(RET_HEADS, 1, RET_V_DIM))


def _out_proj_kernel(o_ref, w_ref, x_ref, gate_ref, y_ref):
    out = jnp.dot(o_ref[...], w_ref[...], preferred_element_type=F32)
    y_ref[...] = x_ref[...] + gate_ref[0] * out


def _out_proj(o, w_bf16, x2, gate, seq):
    n, d = x2.shape
    width = o.shape[1]
    tm = OUT_PROJ_ROWS
    assert n % tm == 0 and seq % tm == 0
    per_seq = seq // tm
    row = lambda i: (i, 0)
    return pl.pallas_call(
        _out_proj_kernel,
        out_shape=jax.ShapeDtypeStruct((n, d), F32),
        grid=(n // tm,),
        in_specs=[
            pl.BlockSpec((tm, width), row),
            _resident(w_bf16.shape),
            pl.BlockSpec((tm, d), row),
            pl.BlockSpec((1, 1, d), lambda i: (i // per_seq, 0, 0)),
        ],
        out_specs=pl.BlockSpec((tm, d), row),
        compiler_params=_params(("arbitrary",)),
        name="out_proj",
    )(o, w_bf16, x2, gate)


def _att_prologue(piece, x_ref, g_ref, shift_ref, scale_ref, h_ref):
    rows = _piece_rows(piece, x_ref.shape[0])
    h = _modulated_norm(x_ref[rows, :], g_ref, shift_ref, scale_ref)
    h_ref[rows, :] = h
    return _zero_from(h)


def _att_project(h_ref, w_ref, qg_ref, kg_ref, q_ref, k_ref, v_ref, sg_ref, next_piece):
    width = ATT_HEADS * ATT_HEAD_DIM
    chunk = 512
    pieces = iter(range(PROLOGUE_PIECES))
    proj = functools.partial(_dot_then_pieces, h_ref, w_ref, pieces=pieces, next_piece=next_piece)

    for out_ref, gain_ref, base, post in ((q_ref, qg_ref, 0, ATT_HEAD_DIM ** -0.5 * LOG2E), (k_ref, kg_ref, width, 1.0)):
        gain = gain_ref[...] * post
        for c in range(0, width, chunk):
            acc = proj(base + c, chunk, count=PIECES_AFTER_BUSY_DOT)
            for hh in range(chunk // ATT_HEAD_DIM):
                a = acc[:, hh * ATT_HEAD_DIM:(hh + 1) * ATT_HEAD_DIM]
                ms = jnp.mean(a * a, axis=-1, keepdims=True)
                out_ref[:, c + hh * ATT_HEAD_DIM:c + (hh + 1) * ATT_HEAD_DIM] = (
                    a * lax.rsqrt(ms + EPS) * gain).astype(BF16)
    for c in range(0, width, chunk):
        sg_ref[:, c:c + chunk] = _silu(proj(3 * width + c, chunk, count=PIECES_AFTER_BUSY_DOT)).astype(BF16)
    for c in range(0, width, chunk):
        v_ref[:, c:c + chunk] = proj(2 * width + c, chunk, count=PIECES_AFTER_CAST_DOT).astype(BF16)
    assert next(pieces, None) is None


def _att_proj_kernel(x0_ref, xn_ref, g_ref, shift0_ref, scale0_ref, shiftn_ref, scalen_ref,
                     w_ref, qg_ref, kg_ref, q_ref, k_ref, v_ref, sg_ref, h_a, h_b):
    step = pl.program_id(0)

    @pl.when(step == 0)
    def _():
        for piece in range(PROLOGUE_PIECES):
            _att_prologue(piece, x0_ref, g_ref, shift0_ref, scale0_ref, h_a)

    def run(cur, nxt):
        next_piece = functools.partial(_att_prologue, x_ref=xn_ref, g_ref=g_ref, shift_ref=shiftn_ref,
                                       scale_ref=scalen_ref, h_ref=nxt)
        _att_project(cur, w_ref, qg_ref, kg_ref, q_ref, k_ref, v_ref, sg_ref, next_piece)

    @pl.when(step % 2 == 0)
    def _():
        run(h_a, h_b)

    @pl.when(step % 2 == 1)
    def _():
        run(h_b, h_a)


def _att_proj(x2, norm_g, shift, scale, w_bf16, q_g, k_g, seq):
    n, d = x2.shape
    tm = PROJ_ROWS
    per_seq = seq // tm
    last = n // tm - 1
    width = ATT_HEADS * ATT_HEAD_DIM
    row = lambda i: (i, 0)
    nxt = lambda i: (jnp.minimum(i + 1, last), 0)
    nxt_bat = lambda i: (jnp.minimum(i + 1, last) // per_seq, 0, 0)
    out = jax.ShapeDtypeStruct((n, width), BF16)
    return pl.pallas_call(
        _att_proj_kernel,
        out_shape=(out, out, out, out),
        grid=(n // tm,),
        in_specs=[
            _resident((tm, d)),
            pl.BlockSpec((tm, d), nxt),
            _resident((1, d)),
            _resident((1, 1, d)),
            _resident((1, 1, d)),
            pl.BlockSpec((1, 1, d), nxt_bat),
            pl.BlockSpec((1, 1, d), nxt_bat),
            _resident(w_bf16.shape),
            _resident((1, ATT_HEAD_DIM)),
            _resident((1, ATT_HEAD_DIM)),
        ],
        out_specs=tuple(pl.BlockSpec((tm, width), row) for _ in range(4)),
        scratch_shapes=[pltpu.VMEM((tm, d), BF16), pltpu.VMEM((tm, d), BF16)],
        compiler_params=_params(("arbitrary",)),
        name="att_proj",
    )(x2, x2, norm_g, shift, scale, shift, scale, w_bf16, q_g, k_g)


def _attend(q, k, v_ext, bias, sg):
    s = lax.dot_general(q, k, (((1,), (1,)), ((), ())), preferred_element_type=F32) + bias
    m = jnp.max(s, axis=-1, keepdims=True)
    p = jnp.exp2(s - m).astype(BF16)
    o_ext = jnp.dot(p, v_ext, preferred_element_type=F32)
    o = o_ext[:, :ATT_HEAD_DIM] / o_ext[:, ATT_HEAD_DIM:]
    return (o * sg.astype(F32)).astype(BF16)


def _attention_kernel(q_ref, k_ref, v_ref, sg_ref, rows_ref, o_ref, bias_ref):
    left = ATT_KWIN - ATT_QBLOCK
    n_blocks = q_ref.shape[0] // ATT_QBLOCK
    first_full = left // ATT_QBLOCK

    qi = lax.broadcasted_iota(jnp.int32, (ATT_QBLOCK, ATT_KWIN), 0)
    kj = lax.broadcasted_iota(jnp.int32, (ATT_QBLOCK, ATT_KWIN), 1)
    q_chunk = (qi + left) // CHUNK
    k_chunk = kj // CHUNK
    band = (k_chunk <= q_chunk) & (k_chunk >= q_chunk - LEFT_CHUNKS)
    for hh in range(2):
        tiled = jnp.broadcast_to(rows_ref[hh], (ATT_QBLOCK, ATT_BIAS_ROW))
        toeplitz = pltpu.roll(tiled, 0, 1, stride=1, stride_axis=0)
        bias_ref[hh] = jnp.where(band, toeplitz[:, :ATT_KWIN], NEG_INF)

    ones = jnp.ones((ATT_KWIN, ATT_HEAD_DIM), BF16)

    def run(row_start, key_start, n_keys):
        rows = pl.ds(row_start, ATT_QBLOCK)
        keys = pl.ds(key_start, n_keys)
        for hh in range(2):
            cols = slice(hh * ATT_HEAD_DIM, (hh + 1) * ATT_HEAD_DIM)
            v_ext = jnp.concatenate([v_ref[keys, cols], ones[:n_keys]], axis=1)
            o_ref[rows, cols] = _attend(
                q_ref[rows, cols], k_ref[keys, cols], v_ext,
                bias_ref[hh, :, ATT_KWIN - n_keys:], sg_ref[rows, cols])

    for blk in range(first_full):
        run(blk * ATT_QBLOCK, 0, (blk + 1) * ATT_QBLOCK)

    def body(blk, carry):
        row_start = pl.multiple_of(blk * ATT_QBLOCK, ATT_QBLOCK)
        run(row_start, pl.multiple_of(row_start - left, ATT_QBLOCK), ATT_KWIN)
        return carry

    lax.fori_loop(first_full, n_blocks, body, 0, unroll=ATT_UNROLL)


def _bias_rows(rel_table):
    left = ATT_KWIN - ATT_QBLOCK
    t = rel_table.astype(F32) * LOG2E
    h = t.shape[0]
    far = jnp.broadcast_to(t[:, -1:], (h, left - MAX_REL))
    near = jnp.broadcast_to(t[:, :1], (h, ATT_KWIN - (left - MAX_REL) - t.shape[1]))
    wrap = jnp.broadcast_to(t[:, -1:], (h, ATT_BIAS_ROW - ATT_KWIN))
    return jnp.concatenate([far, t[:, ::-1], near, wrap], axis=1).reshape(h, 1, ATT_BIAS_ROW)


def _attention(q, k, v, sg, rel_table, batch, seq):
    n, width = q.shape
    assert seq % ATT_QBLOCK == 0
    pair = 2 * ATT_HEAD_DIM
    seq_pair = pl.BlockSpec((seq, pair), lambda b, h: (b, h))
    return pl.pallas_call(
        _attention_kernel,
        out_shape=jax.ShapeDtypeStruct((n, width), BF16),
        grid=(batch, ATT_HEADS // 2),
        in_specs=[seq_pair, seq_pair, seq_pair, seq_pair,
                  pl.BlockSpec((2, 1, ATT_BIAS_ROW), lambda b, h: (h, 0, 0))],
        out_specs=seq_pair,
        scratch_shapes=[pltpu.VMEM((2, ATT_QBLOCK, ATT_KWIN), F32)],
        compiler_params=_params(("arbitrary", "arbitrary")),
        name="attention",
    )(q, k, v, sg, _bias_rows(rel_table))


def kernel(x, c, positions, norm_g, ada_w, ada_b, ret_w_in, ret_gn_g, ret_w_out,
           att_w_in, att_q_g, att_k_g, att_rel_bias, att_w_out):
    batch, seq, d = x.shape
    n = batch * seq
    mod = _adaln(c, ada_w, ada_b)
    shift = mod[:, :, :d].reshape(-1, batch, 1, d)
    scale = mod[:, :, d:2 * d].reshape(-1, batch, 1, d)
    gate = mod[:, :, 2 * d:].reshape(-1, batch, 1, d)

    x2 = x.reshape(n, d)
    pos2 = positions.reshape(n, 1)
    freq = (1.0 / (ROPE_BASE ** (jnp.arange(0, RET_QK_DIM, 2, dtype=F32) / RET_QK_DIM))).reshape(1, -1)

    layer0 = functools.partial(_ret_layer, x2, pos2, freq, norm_g[0].reshape(1, d), shift[0], scale[0],
                               ret_w_in[0].astype(BF16), ret_gn_g[0], seq)
    small = jnp.max(jnp.abs(positions)) < SMALL_ANGLE_LIMIT
    o = lax.cond(small, lambda: layer0(True), lambda: layer0(False))
    x2 = _out_proj(o, ret_w_out[0].astype(BF16), x2, gate[0], seq)

    q, k, v, sg = _att_proj(x2, norm_g[1].reshape(1, d), shift[1], scale[1],
                            att_w_in[0].astype(BF16), att_q_g[0].reshape(1, -1),
                            att_k_g[0].reshape(1, -1), seq)
    o = _attention(q, k, v, sg, att_rel_bias[0], batch, seq)
    x2 = _out_proj(o, att_w_out[0].astype(BF16), x2, gate[1], seq)
    return x2.reshape(batch, seq, d)
```

```python
import functools
import itertools
import math

import jax
import jax.numpy as jnp
from jax import lax
from jax.experimental import pallas as pl
from jax.experimental.pallas import tpu as pltpu

EPS = 1e-6
CHUNK = 64

RET_HEADS = 4
RET_QK_DIM = 256
RET_V_DIM = 512
ROPE_BASE = 10000.0

ATT_HEADS = 16
ATT_HEAD_DIM = 128
LEFT_CHUNKS = 8
MAX_REL = 2 * CHUNK
NEG_INF = -1e30

PROJ_ROWS = 512
OUT_PROJ_ROWS = 1024
RET_BLOCK = 256
RET_STEP_ROWS = 4096
RET_UNROLL = 4
ATT_QBLOCK = 256
ATT_UNROLL = 5
ATT_KWIN = ATT_QBLOCK + LEFT_CHUNKS * CHUNK
ATT_BIAS_ROW = 1024
LOG2E = math.log2(math.e)

VMEM_LIMIT_BYTES = 56 * 1024 * 1024

BF16 = jnp.bfloat16
F32 = jnp.float32


def _resident(shape):
    return pl.BlockSpec(shape, lambda *_: (0,) * len(shape), pipeline_mode=pl.Buffered(1))


def _params(semantics):
    return pltpu.CompilerParams(dimension_semantics=semantics, vmem_limit_bytes=VMEM_LIMIT_BYTES)


def _adaln_kernel(c_ref, w_ref, b_ref, o_ref):
    c = c_ref[...]
    cond = c * (1.0 / (1.0 + jnp.exp(-c)))
    o_ref[0] = jnp.dot(cond, w_ref[0], preferred_element_type=F32) + b_ref[0]


def _adaln(c, ada_w, ada_b):
    depth, d, d3 = ada_w.shape
    b = c.shape[0]
    col = 1024
    return pl.pallas_call(
        _adaln_kernel,
        out_shape=jax.ShapeDtypeStruct((depth, b, d3), F32),
        grid=(depth, d3 // col),
        in_specs=[
            pl.BlockSpec((b, d), lambda i, j: (0, 0)),
            pl.BlockSpec((1, d, col), lambda i, j: (i, 0, j)),
            pl.BlockSpec((1, 1, col), lambda i, j: (i, 0, j)),
        ],
        out_specs=pl.BlockSpec((1, b, col), lambda i, j: (i, 0, j)),
        compiler_params=_params(("arbitrary", "arbitrary")),
        name="adaln",
    )(c, ada_w, ada_b.reshape(depth, 1, d3))


PIECES_AFTER_BUSY_DOT = 1
PIECES_AFTER_CAST_DOT = 5
PROLOGUE_PIECES = 12 * PIECES_AFTER_BUSY_DOT + 4 * PIECES_AFTER_CAST_DOT


def _modulated_norm(x, g_ref, shift_ref, scale_ref):
    ms = jnp.mean(x * x, axis=-1, keepdims=True)
    h = x * lax.rsqrt(ms + EPS) * g_ref[...]
    h = h * (1.0 + scale_ref[0]) + shift_ref[0]
    return h.astype(BF16)


def _piece_rows(piece, tile_rows):
    n = tile_rows // PROLOGUE_PIECES
    return slice(piece * n, (piece + 1) * n)


def _silu(t):
    half = 0.5 * t
    return half + half * jnp.tanh(half)


def _zero_from(*tiles):
    acc = None
    for t in tiles:
        bits = pltpu.bitcast(t, jnp.uint32)
        for r in range(0, bits.shape[0], 8):
            for c in range(0, bits.shape[1], 128):
                part = bits[r:r + 8, c:c + 128]
                acc = part if acc is None else acc | part
    return ((acc >> 16) >> 16).astype(F32)


def _dot_then_pieces(h_ref, w_ref, col, width, pieces, count, next_piece):
    acc = jnp.dot(h_ref[...], w_ref[:, col:col + width], preferred_element_type=F32)
    for piece in itertools.islice(pieces, count):
        top = jnp.concatenate([acc[:8, :128] + next_piece(piece), acc[:8, 128:]], axis=1)
        acc = jnp.concatenate([top, acc[8:]], axis=0)
    return acc


HALF_PI_PARTS = (1.578125, -0.00732421875, -4.470348358154297e-06, 1.5893254712295857e-08)
SMALL_ANGLE_LIMIT = 2 ** 17
SIN_TAYLOR = tuple((-1.0) ** i / math.factorial(2 * i + 1) for i in range(8))
COS_TAYLOR = tuple((-1.0) ** i / math.factorial(2 * i) for i in range(8))


def _sincos_small(x):
    k = jnp.floor(x * (1.0 / math.pi) + 0.5)
    r = x
    for part in HALF_PI_PARTS:
        r = r - k * (2.0 * part)
    z = r * r
    s = SIN_TAYLOR[-1]
    c = COS_TAYLOR[-1]
    for s_coef, c_coef in zip(SIN_TAYLOR[-2::-1], COS_TAYLOR[-2::-1]):
        s = s * z + s_coef
        c = c * z + c_coef
    sign = 1.0 - 2.0 * (k - 2.0 * jnp.floor(0.5 * k))
    return sign * c, sign * (s * r)


def _ret_prologue(piece, x_ref, pos_ref, freq_ref, g_ref, shift_ref, scale_ref, h_ref, cos_ref, sin_ref,
                  small_angles):
    rows = _piece_rows(piece, x_ref.shape[0])
    h = _modulated_norm(x_ref[rows, :], g_ref, shift_ref, scale_ref)
    h_ref[rows, :] = h
    ang = pos_ref[rows, :].astype(F32) * freq_ref[...]
    if small_angles:
        cos, sin = _sincos_small(ang)
    else:
        cos = jnp.cos(ang)
        sin = jnp.sin(ang)
    cos_ref[rows, :] = cos
    sin_ref[rows, :] = sin
    return _zero_from(h, cos, sin)


def _ret_project(h_ref, cos_ref, sin_ref, w_ref, q_ref, k_ref, v_ref, sg_ref, next_piece):
    cos = cos_ref[...]
    sin = sin_ref[...]
    half = RET_QK_DIM // 2
    qk_width = RET_HEADS * RET_QK_DIM
    v_width = RET_HEADS * RET_V_DIM
    pieces = iter(range(PROLOGUE_PIECES))
    proj = functools.partial(_dot_then_pieces, h_ref, w_ref, pieces=pieces, next_piece=next_piece)

    for out_ref, base in ((q_ref, 0), (k_ref, qk_width)):
        for hh in range(RET_HEADS):
            acc = proj(base + hh * RET_QK_DIM, RET_QK_DIM, count=PIECES_AFTER_BUSY_DOT)
            t1 = acc[:, :half]
            t2 = acc[:, half:]
            out_ref[:, hh * RET_QK_DIM:hh * RET_QK_DIM + half] = (t1 * cos - t2 * sin).astype(BF16)
            out_ref[:, hh * RET_QK_DIM + half:(hh + 1) * RET_QK_DIM] = (t1 * sin + t2 * cos).astype(BF16)
    for c in range(0, v_width, RET_V_DIM):
        acc = proj(2 * qk_width + v_width + c, RET_V_DIM, count=PIECES_AFTER_BUSY_DOT)
        sg_ref[:, c:c + RET_V_DIM] = _silu(acc).astype(BF16)
    for c in range(0, v_width, RET_V_DIM):
        acc = proj(2 * qk_width + c, RET_V_DIM, count=PIECES_AFTER_CAST_DOT)
        v_ref[:, c:c + RET_V_DIM] = acc.astype(BF16)
    assert next(pieces, None) is None


def _ret_proj_kernel(x0_ref, xn_ref, pos0_ref, posn_ref, freq_ref, g_ref,
                     shift0_ref, scale0_ref, shiftn_ref, scalen_ref, w_ref,
                     q_ref, k_ref, v_ref, sg_ref,
                     h_a, cos_a, sin_a, h_b, cos_b, sin_b, *, small_angles):
    step = pl.program_id(0)

    @pl.when(step == 0)
    def _():
        for piece in range(PROLOGUE_PIECES):
            _ret_prologue(piece, x0_ref, pos0_ref, freq_ref, g_ref, shift0_ref, scale0_ref, h_a, cos_a, sin_a,
                          small_angles)

    def run(cur, nxt):
        next_piece = functools.partial(_ret_prologue, x_ref=xn_ref, pos_ref=posn_ref, freq_ref=freq_ref,
                                       g_ref=g_ref, shift_ref=shiftn_ref, scale_ref=scalen_ref,
                                       h_ref=nxt[0], cos_ref=nxt[1], sin_ref=nxt[2], small_angles=small_angles)
        _ret_project(*cur, w_ref, q_ref, k_ref, v_ref, sg_ref, next_piece)

    set_a = (h_a, cos_a, sin_a)
    set_b = (h_b, cos_b, sin_b)

    @pl.when(step % 2 == 0)
    def _():
        run(set_a, set_b)

    @pl.when(step % 2 == 1)
    def _():
        run(set_b, set_a)


def _ret_proj(x2, pos2, freq, norm_g, shift, scale, w_bf16, seq, small_angles):
    n, d = x2.shape
    tm = PROJ_ROWS
    assert n % tm == 0 and seq % tm == 0
    per_seq = seq // tm
    last = n // tm - 1
    qk_width = RET_HEADS * RET_QK_DIM
    v_width = RET_HEADS * RET_V_DIM
    half = RET_QK_DIM // 2
    row = lambda i: (i, 0)
    nxt = lambda i: (jnp.minimum(i + 1, last), 0)
    nxt_bat = lambda i: (jnp.minimum(i + 1, last) // per_seq, 0, 0)
    return pl.pallas_call(
        functools.partial(_ret_proj_kernel, small_angles=small_angles),
        out_shape=(
            jax.ShapeDtypeStruct((n, qk_width), BF16),
            jax.ShapeDtypeStruct((n, qk_width), BF16),
            jax.ShapeDtypeStruct((n, v_width), BF16),
            jax.ShapeDtypeStruct((n, v_width), BF16),
        ),
        grid=(n // tm,),
        in_specs=[
            _resident((tm, d)),
            pl.BlockSpec((tm, d), nxt),
            _resident((tm, 1)),
            pl.BlockSpec((tm, 1), nxt),
            _resident((1, half)),
            _resident((1, d)),
            _resident((1, 1, d)),
            _resident((1, 1, d)),
            pl.BlockSpec((1, 1, d), nxt_bat),
            pl.BlockSpec((1, 1, d), nxt_bat),
            _resident(w_bf16.shape),
        ],
        out_specs=(
            pl.BlockSpec((tm, qk_width), row),
            pl.BlockSpec((tm, qk_width), row),
            pl.BlockSpec((tm, v_width), row),
            pl.BlockSpec((tm, v_width), row),
        ),
        scratch_shapes=[pltpu.VMEM((tm, d), BF16), pltpu.VMEM((tm, half), F32), pltpu.VMEM((tm, half), F32),
                        pltpu.VMEM((tm, d), BF16), pltpu.VMEM((tm, half), F32), pltpu.VMEM((tm, half), F32)],
        compiler_params=_params(("arbitrary",)),
        name="ret_proj",
    )(x2, x2, pos2, pos2, freq, norm_g, shift, scale, shift, scale, w_bf16)


def _retention_kernel(q_ref, k_ref, v_ref, sg_ref, dmask_ref, qd_ref, kd_ref, bd_ref, gn_ref,
                      o_ref, state_ref):
    @pl.when(pl.program_id(2) == 0)
    def _():
        state_ref[...] = jnp.zeros_like(state_ref)

    dmask = dmask_ref[0]
    qd = qd_ref[0]
    kd = kd_ref[0]
    decay = bd_ref[0]
    gn = gn_ref[0]

    def body(blk, carry):
        rows = pl.ds(pl.multiple_of(blk * RET_BLOCK, RET_BLOCK), RET_BLOCK)
        q = q_ref[rows, :]
        k = k_ref[rows, :]
        v = v_ref[rows, :]
        state = state_ref[...]
        scores = lax.dot_general(q, k, (((1,), (1,)), ((), ())), preferred_element_type=F32)
        o = jnp.dot((scores * dmask).astype(BF16), v, preferred_element_type=F32)
        qs = (q.astype(F32) * qd).astype(BF16)
        o = o + jnp.dot(qs, state.astype(BF16), preferred_element_type=F32)
        ks = (k.astype(F32) * kd).astype(BF16)
        state_ref[...] = decay * state + lax.dot_general(
            ks, v, (((0,), (0,)), ((), ())), preferred_element_type=F32)
        ms = jnp.mean(o * o, axis=-1, keepdims=True)
        o = o * lax.rsqrt(ms + EPS) * gn
        o_ref[rows, :] = (o * sg_ref[rows, :].astype(F32)).astype(BF16)
        return carry

    lax.fori_loop(0, RET_STEP_ROWS // RET_BLOCK, body, 0, unroll=RET_UNROLL)


def _retention_tables():
    log_gamma = jnp.log(1.0 - 2.0 ** (-5.0 - jnp.arange(RET_HEADS, dtype=F32)))
    t = jnp.arange(RET_BLOCK, dtype=F32)
    dist = t[:, None] - t[None, :]
    chunk = jnp.arange(RET_BLOCK) // CHUNK
    visible = chunk[None, :] <= chunk[:, None]
    k_scale = RET_QK_DIM ** -0.5
    dmask = jnp.where(visible[None], jnp.exp(log_gamma[:, None, None] * jnp.abs(dist)[None]), 0.0) * k_scale
    qd = jnp.exp(log_gamma[:, None] * (t + 1.0))
    kd = jnp.exp(log_gamma[:, None] * (RET_BLOCK - 1.0 - t)) * k_scale
    qd = jnp.broadcast_to(qd[:, :, None], (RET_HEADS, RET_BLOCK, RET_QK_DIM))
    kd = jnp.broadcast_to(kd[:, :, None], (RET_HEADS, RET_BLOCK, RET_QK_DIM))
    bd = jnp.broadcast_to(jnp.exp(log_gamma * RET_BLOCK)[:, None, None], (RET_HEADS, 1, RET_V_DIM))
    return dmask.astype(F32), qd.astype(F32), kd.astype(F32), bd.astype(F32)


def _retention(q, k, v, sg, gn_g, batch, seq):
    n = q.shape[0]
    rows = RET_STEP_ROWS
    assert seq % rows == 0
    per_seq = seq // rows
    dmask, qd, kd, bd = _retention_tables()
    row_head = lambda b, h, j: (b * per_seq + j, h)
    head_tab = lambda b, h, j: (h, 0, 0)
    return pl.pallas_call(
        _retention_kernel,
        out_shape=jax.ShapeDtypeStruct((n, RET_HEADS * RET_V_DIM), BF16),
        grid=(batch, RET_HEADS, per_seq),
        in_specs=[
            pl.BlockSpec((rows, RET_QK_DIM), row_head),
            pl.BlockSpec((rows, RET_QK_DIM), row_head),
            pl.BlockSpec((rows, RET_V_DIM), row_head),
            pl.BlockSpec((rows, RET_V_DIM), row_head),
            pl.BlockSpec((1, RET_BLOCK, RET_BLOCK), head_tab),
            pl.BlockSpec((1, RET_BLOCK, RET_QK_DIM), head_tab),
            pl.BlockSpec((1, RET_BLOCK, RET_QK_DIM), head_tab),
            pl.BlockSpec((1, 1, RET_V_DIM), head_tab),
            pl.BlockSpec((1, 1, RET_V_DIM), head_tab),
        ],
        out_specs=pl.BlockSpec((rows, RET_V_DIM), row_head),
        scratch_shapes=[pltpu.VMEM((RET_QK_DIM, RET_V_DIM), F32)],
        compiler_params=_params(("arbitrary", "arbitrary", "arbitrary")),
        name="retention",
    )(q, k, v, sg, dmask, qd, kd, bd, gn_g.reshape(RET_HEADS, 1, RET_V_DIM))


def _out_proj_kernel(o_ref, w_ref, x_ref, gate_ref, y_ref):
    out = jnp.dot(o_ref[...], w_ref[...], preferred_element_type=F32)
    y_ref[...] = x_ref[...] + gate_ref[0] * out


def _out_proj(o, w_bf16, x2, gate, seq):
    n, d = x2.shape
    width = o.shape[1]
    tm = OUT_PROJ_ROWS
    assert n % tm == 0 and seq % tm == 0
    per_seq = seq // tm
    row = lambda i: (i, 0)
    return pl.pallas_call(
        _out_proj_kernel,
        out_shape=jax.ShapeDtypeStruct((n, d), F32),
        grid=(n // tm,),
        in_specs=[
            pl.BlockSpec((tm, width), row),
            _resident(w_bf16.shape),
            pl.BlockSpec((tm, d), row),
            pl.BlockSpec((1, 1, d), lambda i: (i // per_seq, 0, 0)),
        ],
        out_specs=pl.BlockSpec((tm, d), row),
        compiler_params=_params(("arbitrary",)),
        name="out_proj",
    )(o, w_bf16, x2, gate)


def _att_prologue(piece, x_ref, g_ref, shift_ref, scale_ref, h_ref):
    rows = _piece_rows(piece, x_ref.shape[0])
    h = _modulated_norm(x_ref[rows, :], g_ref, shift_ref, scale_ref)
    h_ref[rows, :] = h
    return _zero_from(h)


def _att_project(h_ref, w_ref, qg_ref, kg_ref, q_ref, k_ref, v_ref, sg_ref, next_piece):
    width = ATT_HEADS * ATT_HEAD_DIM
    chunk = 512
    pieces = iter(range(PROLOGUE_PIECES))
    proj = functools.partial(_dot_then_pieces, h_ref, w_ref, pieces=pieces, next_piece=next_piece)

    for out_ref, gain_ref, base, post in ((q_ref, qg_ref, 0, ATT_HEAD_DIM ** -0.5 * LOG2E), (k_ref, kg_ref, width, 1.0)):
        gain = gain_ref[...] * post
        for c in range(0, width, chunk):
            acc = proj(base + c, chunk, count=PIECES_AFTER_BUSY_DOT)
            for hh in range(chunk // ATT_HEAD_DIM):
                a = acc[:, hh * ATT_HEAD_DIM:(hh + 1) * ATT_HEAD_DIM]
                ms = jnp.mean(a * a, axis=-1, keepdims=True)
                out_ref[:, c + hh * ATT_HEAD_DIM:c + (hh + 1) * ATT_HEAD_DIM] = (
                    a * lax.rsqrt(ms + EPS) * gain).astype(BF16)
    for c in range(0, width, chunk):
        sg_ref[:, c:c + chunk] = _silu(proj(3 * width + c, chunk, count=PIECES_AFTER_BUSY_DOT)).astype(BF16)
    for c in range(0, width, chunk):
        v_ref[:, c:c + chunk] = proj(2 * width + c, chunk, count=PIECES_AFTER_CAST_DOT).astype(BF16)
    assert next(pieces, None) is None


def _att_proj_kernel(x0_ref, xn_ref, g_ref, shift0_ref, scale0_ref, shiftn_ref, scalen_ref,
                     w_ref, qg_ref, kg_ref, q_ref, k_ref, v_ref, sg_ref, h_a, h_b):
    step = pl.program_id(0)

    @pl.when(step == 0)
    def _():
        for piece in range(PROLOGUE_PIECES):
            _att_prologue(piece, x0_ref, g_ref, shift0_ref, scale0_ref, h_a)

    def run(cur, nxt):
        next_piece = functools.partial(_att_prologue, x_ref=xn_ref, g_ref=g_ref, shift_ref=shiftn_ref,
                                       scale_ref=scalen_ref, h_ref=nxt)
        _att_project(cur, w_ref, qg_ref, kg_ref, q_ref, k_ref, v_ref, sg_ref, next_piece)

    @pl.when(step % 2 == 0)
    def _():
        run(h_a, h_b)

    @pl.when(step % 2 == 1)
    def _():
        run(h_b, h_a)


def _att_proj(x2, norm_g, shift, scale, w_bf16, q_g, k_g, seq):
    n, d = x2.shape
    tm = PROJ_ROWS
    per_seq = seq // tm
    last = n // tm - 1
    width = ATT_HEADS * ATT_HEAD_DIM
    row = lambda i: (i, 0)
    nxt = lambda i: (jnp.minimum(i + 1, last), 0)
    nxt_bat = lambda i: (jnp.minimum(i + 1, last) // per_seq, 0, 0)
    out = jax.ShapeDtypeStruct((n, width), BF16)
    return pl.pallas_call(
        _att_proj_kernel,
        out_shape=(out, out, out, out),
        grid=(n // tm,),
        in_specs=[
            _resident((tm, d)),
            pl.BlockSpec((tm, d), nxt),
            _resident((1, d)),
            _resident((1, 1, d)),
            _resident((1, 1, d)),
            pl.BlockSpec((1, 1, d), nxt_bat),
            pl.BlockSpec((1, 1, d), nxt_bat),
            _resident(w_bf16.shape),
            _resident((1, ATT_HEAD_DIM)),
            _resident((1, ATT_HEAD_DIM)),
        ],
        out_specs=tuple(pl.BlockSpec((tm, width), row) for _ in range(4)),
        scratch_shapes=[pltpu.VMEM((tm, d), BF16), pltpu.VMEM((tm, d), BF16)],
        compiler_params=_params(("arbitrary",)),
        name="att_proj",
    )(x2, x2, norm_g, shift, scale, shift, scale, w_bf16, q_g, k_g)


def _attend(q, k, v_ext, bias, sg):
    s = lax.dot_general(q, k, (((1,), (1,)), ((), ())), preferred_element_type=F32) + bias
    m = jnp.max(s, axis=-1, keepdims=True)
    p = jnp.exp2(s - m).astype(BF16)
    o_ext = jnp.dot(p, v_ext, preferred_element_type=F32)
    o = o_ext[:, :ATT_HEAD_DIM] / o_ext[:, ATT_HEAD_DIM:]
    return (o * sg.astype(F32)).astype(BF16)


def _attention_kernel(q_ref, k_ref, v_ref, sg_ref, rows_ref, o_ref, bias_ref):
    left = ATT_KWIN - ATT_QBLOCK
    n_blocks = q_ref.shape[0] // ATT_QBLOCK
    first_full = left // ATT_QBLOCK

    qi = lax.broadcasted_iota(jnp.int32, (ATT_QBLOCK, ATT_KWIN), 0)
    kj = lax.broadcasted_iota(jnp.int32, (ATT_QBLOCK, ATT_KWIN), 1)
    q_chunk = (qi + left) // CHUNK
    k_chunk = kj // CHUNK
    band = (k_chunk <= q_chunk) & (k_chunk >= q_chunk - LEFT_CHUNKS)
    for hh in range(2):
        tiled = jnp.broadcast_to(rows_ref[hh], (ATT_QBLOCK, ATT_BIAS_ROW))
        toeplitz = pltpu.roll(tiled, 0, 1, stride=1, stride_axis=0)
        bias_ref[hh] = jnp.where(band, toeplitz[:, :ATT_KWIN], NEG_INF)

    ones = jnp.ones((ATT_KWIN, ATT_HEAD_DIM), BF16)

    def run(row_start, key_start, n_keys):
        rows = pl.ds(row_start, ATT_QBLOCK)
        keys = pl.ds(key_start, n_keys)
        for hh in range(2):
            cols = slice(hh * ATT_HEAD_DIM, (hh + 1) * ATT_HEAD_DIM)
            v_ext = jnp.concatenate([v_ref[keys, cols], ones[:n_keys]], axis=1)
            o_ref[rows, cols] = _attend(
                q_ref[rows, cols], k_ref[keys, cols], v_ext,
                bias_ref[hh, :, ATT_KWIN - n_keys:], sg_ref[rows, cols])

    for blk in range(first_full):
        run(blk * ATT_QBLOCK, 0, (blk + 1) * ATT_QBLOCK)

    def body(blk, carry):
        row_start = pl.multiple_of(blk * ATT_QBLOCK, ATT_QBLOCK)
        run(row_start, pl.multiple_of(row_start - left, ATT_QBLOCK), ATT_KWIN)
        return carry

    lax.fori_loop(first_full, n_blocks, body, 0, unroll=ATT_UNROLL)


def _bias_rows(rel_table):
    left = ATT_KWIN - ATT_QBLOCK
    t = rel_table.astype(F32) * LOG2E
    h = t.shape[0]
    far = jnp.broadcast_to(t[:, -1:], (h, left - MAX_REL))
    near = jnp.broadcast_to(t[:, :1], (h, ATT_KWIN - (left - MAX_REL) - t.shape[1]))
    wrap = jnp.broadcast_to(t[:, -1:], (h, ATT_BIAS_ROW - ATT_KWIN))
    return jnp.concatenate([far, t[:, ::-1], near, wrap], axis=1).reshape(h, 1, ATT_BIAS_ROW)


def _attention(q, k, v, sg, rel_table, batch, seq):
    n, width = q.shape
    assert seq % ATT_QBLOCK == 0
    pair = 2 * ATT_HEAD_DIM
    seq_pair = pl.BlockSpec((seq, pair), lambda b, h: (b, h))
    return pl.pallas_call(
        _attention_kernel,
        out_shape=jax.ShapeDtypeStruct((n, width), BF16),
        grid=(batch, ATT_HEADS // 2),
        in_specs=[seq_pair, seq_pair, seq_pair, seq_pair,
                  pl.BlockSpec((2, 1, ATT_BIAS_ROW), lambda b, h: (h, 0, 0))],
        out_specs=seq_pair,
        scratch_shapes=[pltpu.VMEM((2, ATT_QBLOCK, ATT_KWIN), F32)],
        compiler_params=_params(("arbitrary", "arbitrary")),
        name="attention",
    )(q, k, v, sg, _bias_rows(rel_table))


def kernel(x, c, positions, norm_g, ada_w, ada_b, ret_w_in, ret_gn_g, ret_w_out,
           att_w_in, att_q_g, att_k_g, att_rel_bias, att_w_out):
    batch, seq, d = x.shape
    n = batch * seq
    mod = _adaln(c, ada_w, ada_b)
    shift = mod[:, :, :d].reshape(-1, batch, 1, d)
    scale = mod[:, :, d:2 * d].reshape(-1, batch, 1, d)
    gate = mod[:, :, 2 * d:].reshape(-1, batch, 1, d)

    x2 = x.reshape(n, d)
    pos2 = positions.reshape(n, 1)
    freq = (1.0 / (ROPE_BASE ** (jnp.arange(0, RET_QK_DIM, 2, dtype=F32) / RET_QK_DIM))).reshape(1, -1)

    proj0 = functools.partial(_ret_proj, x2, pos2, freq, norm_g[0].reshape(1, d), shift[0], scale[0],
                              ret_w_in[0].astype(BF16), seq)
    small = jnp.max(jnp.abs(positions)) < SMALL_ANGLE_LIMIT
    q, k, v, sg = lax.cond(small, lambda: proj0(True), lambda: proj0(False))
    o = _retention(q, k, v, sg, ret_gn_g[0], batch, seq)
    x2 = _out_proj(o, ret_w_out[0].astype(BF16), x2, gate[0], seq)

    q, k, v, sg = _att_proj(x2, norm_g[1].reshape(1, d), shift[1], scale[1],
                            att_w_in[0].astype(BF16), att_q_g[0].reshape(1, -1),
                            att_k_g[0].reshape(1, -1), seq)
    o = _attention(q, k, v, sg, att_rel_bias[0], batch, seq)
    x2 = _out_proj(o, att_w_out[0].astype(BF16), x2, gate[1], seq)
    return x2.reshape(batch, seq, d)
```

```python
import functools
import itertools
import math

import jax
import jax.numpy as jnp
from jax import lax
from jax.experimental import pallas as pl
from jax.experimental.pallas import tpu as pltpu

EPS = 1e-6
CHUNK = 64

RET_HEADS = 4
RET_QK_DIM = 256
RET_V_DIM = 512
ROPE_BASE = 10000.0

ATT_HEADS = 16
ATT_HEAD_DIM = 128
LEFT_CHUNKS = 8
MAX_REL = 2 * CHUNK
NEG_INF = -1e30

PROJ_ROWS = 512
OUT_PROJ_ROWS = 1024
RET_BLOCK = 256
RET_STEP_ROWS = 4096
RET_UNROLL = 4
ATT_QBLOCK = 256
ATT_UNROLL = 5
ATT_KWIN = ATT_QBLOCK + LEFT_CHUNKS * CHUNK
ATT_BIAS_ROW = 1024
LOG2E = math.log2(math.e)

VMEM_LIMIT_BYTES = 56 * 1024 * 1024

BF16 = jnp.bfloat16
F32 = jnp.float32


def _resident(shape):
    return pl.BlockSpec(shape, lambda *_: (0,) * len(shape), pipeline_mode=pl.Buffered(1))


def _params(semantics):
    return pltpu.CompilerParams(dimension_semantics=semantics, vmem_limit_bytes=VMEM_LIMIT_BYTES)


def _adaln_kernel(c_ref, w_ref, b_ref, o_ref):
    c = c_ref[...]
    cond = c * (1.0 / (1.0 + jnp.exp(-c)))
    o_ref[0] = jnp.dot(cond, w_ref[0], preferred_element_type=F32) + b_ref[0]


def _adaln(c, ada_w, ada_b):
    depth, d, d3 = ada_w.shape
    b = c.shape[0]
    col = 1024
    return pl.pallas_call(
        _adaln_kernel,
        out_shape=jax.ShapeDtypeStruct((depth, b, d3), F32),
        grid=(depth, d3 // col),
        in_specs=[
            pl.BlockSpec((b, d), lambda i, j: (0, 0)),
            pl.BlockSpec((1, d, col), lambda i, j: (i, 0, j)),
            pl.BlockSpec((1, 1, col), lambda i, j: (i, 0, j)),
        ],
        out_specs=pl.BlockSpec((1, b, col), lambda i, j: (i, 0, j)),
        compiler_params=_params(("arbitrary", "arbitrary")),
        name="adaln",
    )(c, ada_w, ada_b.reshape(depth, 1, d3))


PIECES_AFTER_BUSY_DOT = 1
PIECES_AFTER_CAST_DOT = 5
PROLOGUE_PIECES = 12 * PIECES_AFTER_BUSY_DOT + 4 * PIECES_AFTER_CAST_DOT


def _modulated_norm(x, g_ref, shift_ref, scale_ref):
    ms = jnp.mean(x * x, axis=-1, keepdims=True)
    h = x * lax.rsqrt(ms + EPS) * g_ref[...]
    h = h * (1.0 + scale_ref[0]) + shift_ref[0]
    return h.astype(BF16)


def _piece_rows(piece, tile_rows):
    n = tile_rows // PROLOGUE_PIECES
    return slice(piece * n, (piece + 1) * n)


def _silu(t):
    half = 0.5 * t
    return half + half * jnp.tanh(half)


def _zero_from(*tiles):
    acc = None
    for t in tiles:
        bits = pltpu.bitcast(t, jnp.uint32)
        for r in range(0, bits.shape[0], 8):
            for c in range(0, bits.shape[1], 128):
                part = bits[r:r + 8, c:c + 128]
                acc = part if acc is None else acc | part
    return ((acc >> 16) >> 16).astype(F32)


def _dot_then_pieces(h_ref, w_ref, col, width, pieces, count, next_piece):
    acc = jnp.dot(h_ref[...], w_ref[:, col:col + width], preferred_element_type=F32)
    for piece in itertools.islice(pieces, count):
        top = jnp.concatenate([acc[:8, :128] + next_piece(piece), acc[:8, 128:]], axis=1)
        acc = jnp.concatenate([top, acc[8:]], axis=0)
    return acc


HALF_PI_PARTS = (1.578125, -0.00732421875, -4.470348358154297e-06, 1.5893254712295857e-08)
SMALL_ANGLE_LIMIT = 2 ** 17
SIN_TAYLOR = tuple((-1.0) ** i / math.factorial(2 * i + 1) for i in range(8))
COS_TAYLOR = tuple((-1.0) ** i / math.factorial(2 * i) for i in range(8))


def _sincos_small(x):
    k = jnp.floor(x * (1.0 / math.pi) + 0.5)
    r = x
    for part in HALF_PI_PARTS:
        r = r - k * (2.0 * part)
    z = r * r
    s = SIN_TAYLOR[-1]
    c = COS_TAYLOR[-1]
    for s_coef, c_coef in zip(SIN_TAYLOR[-2::-1], COS_TAYLOR[-2::-1]):
        s = s * z + s_coef
        c = c * z + c_coef
    sign = 1.0 - 2.0 * (k - 2.0 * jnp.floor(0.5 * k))
    return sign * c, sign * (s * r)


def _ret_prologue(piece, x_ref, pos_ref, freq_ref, g_ref, shift_ref, scale_ref, h_ref, cos_ref, sin_ref,
                  small_angles):
    rows = _piece_rows(piece, x_ref.shape[0])
    h = _modulated_norm(x_ref[rows, :], g_ref, shift_ref, scale_ref)
    h_ref[rows, :] = h
    ang = pos_ref[rows, :].astype(F32) * freq_ref[...]
    if small_angles:
        cos, sin = _sincos_small(ang)
    else:
        cos = jnp.cos(ang)
        sin = jnp.sin(ang)
    cos_ref[rows, :] = cos
    sin_ref[rows, :] = sin
    return _zero_from(h, cos, sin)


def _ret_project(h_ref, cos_ref, sin_ref, w_ref, q_ref, k_ref, v_ref, sg_ref, next_piece):
    cos = cos_ref[...]
    sin = sin_ref[...]
    half = RET_QK_DIM // 2
    qk_width = RET_HEADS * RET_QK_DIM
    v_width = RET_HEADS * RET_V_DIM
    pieces = iter(range(PROLOGUE_PIECES))
    proj = functools.partial(_dot_then_pieces, h_ref, w_ref, pieces=pieces, next_piece=next_piece)

    for out_ref, base in ((q_ref, 0), (k_ref, qk_width)):
        for hh in range(RET_HEADS):
            acc = proj(base + hh * RET_QK_DIM, RET_QK_DIM, count=PIECES_AFTER_BUSY_DOT)
            t1 = acc[:, :half]
            t2 = acc[:, half:]
            out_ref[:, hh * RET_QK_DIM:hh * RET_QK_DIM + half] = (t1 * cos - t2 * sin).astype(BF16)
            out_ref[:, hh * RET_QK_DIM + half:(hh + 1) * RET_QK_DIM] = (t1 * sin + t2 * cos).astype(BF16)
    for c in range(0, v_width, RET_V_DIM):
        acc = proj(2 * qk_width + v_width + c, RET_V_DIM, count=PIECES_AFTER_BUSY_DOT)
        sg_ref[:, c:c + RET_V_DIM] = _silu(acc).astype(BF16)
    for c in range(0, v_width, RET_V_DIM):
        acc = proj(2 * qk_width + c, RET_V_DIM, count=PIECES_AFTER_CAST_DOT)
        v_ref[:, c:c + RET_V_DIM] = acc.astype(BF16)
    assert next(pieces, None) is None


def _ret_proj_kernel(x0_ref, xn_ref, pos0_ref, posn_ref, freq_ref, g_ref,
                     shift0_ref, scale0_ref, shiftn_ref, scalen_ref, w_ref,
                     q_ref, k_ref, v_ref, sg_ref,
                     h_a, cos_a, sin_a, h_b, cos_b, sin_b, *, small_angles):
    step = pl.program_id(0)

    @pl.when(step == 0)
    def _():
        for piece in range(PROLOGUE_PIECES):
            _ret_prologue(piece, x0_ref, pos0_ref, freq_ref, g_ref, shift0_ref, scale0_ref, h_a, cos_a, sin_a,
                          small_angles)

    def run(cur, nxt):
        next_piece = functools.partial(_ret_prologue, x_ref=xn_ref, pos_ref=posn_ref, freq_ref=freq_ref,
                                       g_ref=g_ref, shift_ref=shiftn_ref, scale_ref=scalen_ref,
                                       h_ref=nxt[0], cos_ref=nxt[1], sin_ref=nxt[2], small_angles=small_angles)
        _ret_project(*cur, w_ref, q_ref, k_ref, v_ref, sg_ref, next_piece)

    set_a = (h_a, cos_a, sin_a)
    set_b = (h_b, cos_b, sin_b)

    @pl.when(step % 2 == 0)
    def _():
        run(set_a, set_b)

    @pl.when(step % 2 == 1)
    def _():
        run(set_b, set_a)


def _ret_proj(x2, pos2, freq, norm_g, shift, scale, w_bf16, seq, small_angles):
    n, d = x2.shape
    tm = PROJ_ROWS
    assert n % tm == 0 and seq % tm == 0
    per_seq = seq // tm
    last = n // tm - 1
    qk_width = RET_HEADS * RET_QK_DIM
    v_width = RET_HEADS * RET_V_DIM
    half = RET_QK_DIM // 2
    row = lambda i: (i, 0)
    nxt = lambda i: (jnp.minimum(i + 1, last), 0)
    nxt_bat = lambda i: (jnp.minimum(i + 1, last) // per_seq, 0, 0)
    return pl.pallas_call(
        functools.partial(_ret_proj_kernel, small_angles=small_angles),
        out_shape=(
            jax.ShapeDtypeStruct((n, qk_width), BF16),
            jax.ShapeDtypeStruct((n, qk_width), BF16),
            jax.ShapeDtypeStruct((n, v_width), BF16),
            jax.ShapeDtypeStruct((n, v_width), BF16),
        ),
        grid=(n // tm,),
        in_specs=[
            _resident((tm, d)),
            pl.BlockSpec((tm, d), nxt),
            _resident((tm, 1)),
            pl.BlockSpec((tm, 1), nxt),
            _resident((1, half)),
            _resident((1, d)),
            _resident((1, 1, d)),
            _resident((1, 1, d)),
            pl.BlockSpec((1, 1, d), nxt_bat),
            pl.BlockSpec((1, 1, d), nxt_bat),
            _resident(w_bf16.shape),
        ],
        out_specs=(
            pl.BlockSpec((tm, qk_width), row),
            pl.BlockSpec((tm, qk_width), row),
            pl.BlockSpec((tm, v_width), row),
            pl.BlockSpec((tm, v_width), row),
        ),
        scratch_shapes=[pltpu.VMEM((tm, d), BF16), pltpu.VMEM((tm, half), F32), pltpu.VMEM((tm, half), F32),
                        pltpu.VMEM((tm, d), BF16), pltpu.VMEM((tm, half), F32), pltpu.VMEM((tm, half), F32)],
        compiler_params=_params(("arbitrary",)),
        name="ret_proj",
    )(x2, x2, pos2, pos2, freq, norm_g, shift, scale, shift, scale, w_bf16)


def _retention_kernel(q_ref, k_ref, v_ref, sg_ref, dmask_ref, qd_ref, kd_ref, bd_ref, gn_ref,
                      o_ref, state_ref):
    @pl.when(pl.program_id(2) == 0)
    def _():
        state_ref[...] = jnp.zeros_like(state_ref)

    dmask = dmask_ref[0]
    qd = qd_ref[0]
    kd = kd_ref[0]
    decay = bd_ref[0]
    gn = gn_ref[0]

    def body(blk, carry):
        rows = pl.ds(pl.multiple_of(blk * RET_BLOCK, RET_BLOCK), RET_BLOCK)
        q = q_ref[rows, :]
        k = k_ref[rows, :]
        v = v_ref[rows, :]
        state = state_ref[...]
        scores = lax.dot_general(q, k, (((1,), (1,)), ((), ())), preferred_element_type=F32)
        o = jnp.dot((scores * dmask).astype(BF16), v, preferred_element_type=F32)
        qs = q * qd
        o = o + jnp.dot(qs, state.astype(BF16), preferred_element_type=F32)
        ks = k * kd
        state_ref[...] = decay * state + lax.dot_general(
            ks, v, (((0,), (0,)), ((), ())), preferred_element_type=F32)
        ms = jnp.mean(o * o, axis=-1, keepdims=True)
        o = o * lax.rsqrt(ms + EPS) * gn
        o_ref[rows, :] = (o * sg_ref[rows, :].astype(F32)).astype(BF16)
        return carry

    lax.fori_loop(0, RET_STEP_ROWS // RET_BLOCK, body, 0, unroll=RET_UNROLL)


def _retention_tables():
    log_gamma = jnp.log(1.0 - 2.0 ** (-5.0 - jnp.arange(RET_HEADS, dtype=F32)))
    t = jnp.arange(RET_BLOCK, dtype=F32)
    dist = t[:, None] - t[None, :]
    chunk = jnp.arange(RET_BLOCK) // CHUNK
    visible = chunk[None, :] <= chunk[:, None]
    k_scale = RET_QK_DIM ** -0.5
    dmask = jnp.where(visible[None], jnp.exp(log_gamma[:, None, None] * jnp.abs(dist)[None]), 0.0) * k_scale
    qd = jnp.exp(log_gamma[:, None] * (t + 1.0))
    kd = jnp.exp(log_gamma[:, None] * (RET_BLOCK - 1.0 - t)) * k_scale
    qd = jnp.broadcast_to(qd[:, :, None], (RET_HEADS, RET_BLOCK, RET_QK_DIM))
    kd = jnp.broadcast_to(kd[:, :, None], (RET_HEADS, RET_BLOCK, RET_QK_DIM))
    bd = jnp.broadcast_to(jnp.exp(log_gamma * RET_BLOCK)[:, None, None], (RET_HEADS, 1, RET_V_DIM))
    return dmask.astype(F32), qd.astype(BF16), kd.astype(BF16), bd.astype(F32)


def _retention(q, k, v, sg, gn_g, batch, seq):
    n = q.shape[0]
    rows = RET_STEP_ROWS
    assert seq % rows == 0
    per_seq = seq // rows
    dmask, qd, kd, bd = _retention_tables()
    row_head = lambda b, h, j: (b * per_seq + j, h)
    head_tab = lambda b, h, j: (h, 0, 0)
    return pl.pallas_call(
        _retention_kernel,
        out_shape=jax.ShapeDtypeStruct((n, RET_HEADS * RET_V_DIM), BF16),
        grid=(batch, RET_HEADS, per_seq),
        in_specs=[
            pl.BlockSpec((rows, RET_QK_DIM), row_head),
            pl.BlockSpec((rows, RET_QK_DIM), row_head),
            pl.BlockSpec((rows, RET_V_DIM), row_head),
            pl.BlockSpec((rows, RET_V_DIM), row_head),
            pl.BlockSpec((1, RET_BLOCK, RET_BLOCK), head_tab),
            pl.BlockSpec((1, RET_BLOCK, RET_QK_DIM), head_tab),
            pl.BlockSpec((1, RET_BLOCK, RET_QK_DIM), head_tab),
            pl.BlockSpec((1, 1, RET_V_DIM), head_tab),
            pl.BlockSpec((1, 1, RET_V_DIM), head_tab),
        ],
        out_specs=pl.BlockSpec((rows, RET_V_DIM), row_head),
        scratch_shapes=[pltpu.VMEM((RET_QK_DIM, RET_V_DIM), F32)],
        compiler_params=_params(("arbitrary", "arbitrary", "arbitrary")),
        name="retention",
    )(q, k, v, sg, dmask, qd, kd, bd, gn_g.reshape(RET_HEADS, 1, RET_V_DIM))


def _out_proj_kernel(o_ref, w_ref, x_ref, gate_ref, y_ref):
    out = jnp.dot(o_ref[...], w_ref[...], preferred_element_type=F32)
    y_ref[...] = x_ref[...] + gate_ref[0] * out


def _out_proj(o, w_bf16, x2, gate, seq):
    n, d = x2.shape
    width = o.shape[1]
    tm = OUT_PROJ_ROWS
    assert n % tm == 0 and seq % tm == 0
    per_seq = seq // tm
    row = lambda i: (i, 0)
    return pl.pallas_call(
        _out_proj_kernel,
        out_shape=jax.ShapeDtypeStruct((n, d), F32),
        grid=(n // tm,),
        in_specs=[
            pl.BlockSpec((tm, width), row),
            _resident(w_bf16.shape),
            pl.BlockSpec((tm, d), row),
            pl.BlockSpec((1, 1, d), lambda i: (i // per_seq, 0, 0)),
        ],
        out_specs=pl.BlockSpec((tm, d), row),
        compiler_params=_params(("arbitrary",)),
        name="out_proj",
    )(o, w_bf16, x2, gate)


def _att_prologue(piece, x_ref, g_ref, shift_ref, scale_ref, h_ref):
    rows = _piece_rows(piece, x_ref.shape[0])
    h = _modulated_norm(x_ref[rows, :], g_ref, shift_ref, scale_ref)
    h_ref[rows, :] = h
    return _zero_from(h)


def _att_project(h_ref, w_ref, qg_ref, kg_ref, q_ref, k_ref, v_ref, sg_ref, next_piece):
    width = ATT_HEADS * ATT_HEAD_DIM
    chunk = 512
    pieces = iter(range(PROLOGUE_PIECES))
    proj = functools.partial(_dot_then_pieces, h_ref, w_ref, pieces=pieces, next_piece=next_piece)

    for out_ref, gain_ref, base, post in ((q_ref, qg_ref, 0, ATT_HEAD_DIM ** -0.5 * LOG2E), (k_ref, kg_ref, width, 1.0)):
        gain = gain_ref[...] * post
        for c in range(0, width, chunk):
            acc = proj(base + c, chunk, count=PIECES_AFTER_BUSY_DOT)
            for hh in range(chunk // ATT_HEAD_DIM):
                a = acc[:, hh * ATT_HEAD_DIM:(hh + 1) * ATT_HEAD_DIM]
                ms = jnp.mean(a * a, axis=-1, keepdims=True)
                out_ref[:, c + hh * ATT_HEAD_DIM:c + (hh + 1) * ATT_HEAD_DIM] = (
                    a * lax.rsqrt(ms + EPS) * gain).astype(BF16)
    for c in range(0, width, chunk):
        sg_ref[:, c:c + chunk] = _silu(proj(3 * width + c, chunk, count=PIECES_AFTER_BUSY_DOT)).astype(BF16)
    for c in range(0, width, chunk):
        v_ref[:, c:c + chunk] = proj(2 * width + c, chunk, count=PIECES_AFTER_CAST_DOT).astype(BF16)
    assert next(pieces, None) is None


def _att_proj_kernel(x0_ref, xn_ref, g_ref, shift0_ref, scale0_ref, shiftn_ref, scalen_ref,
                     w_ref, qg_ref, kg_ref, q_ref, k_ref, v_ref, sg_ref, h_a, h_b):
    step = pl.program_id(0)

    @pl.when(step == 0)
    def _():
        for piece in range(PROLOGUE_PIECES):
            _att_prologue(piece, x0_ref, g_ref, shift0_ref, scale0_ref, h_a)

    def run(cur, nxt):
        next_piece = functools.partial(_att_prologue, x_ref=xn_ref, g_ref=g_ref, shift_ref=shiftn_ref,
                                       scale_ref=scalen_ref, h_ref=nxt)
        _att_project(cur, w_ref, qg_ref, kg_ref, q_ref, k_ref, v_ref, sg_ref, next_piece)

    @pl.when(step % 2 == 0)
    def _():
        run(h_a, h_b)

    @pl.when(step % 2 == 1)
    def _():
        run(h_b, h_a)


def _att_proj(x2, norm_g, shift, scale, w_bf16, q_g, k_g, seq):
    n, d = x2.shape
    tm = PROJ_ROWS
    per_seq = seq // tm
    last = n // tm - 1
    width = ATT_HEADS * ATT_HEAD_DIM
    row = lambda i: (i, 0)
    nxt = lambda i: (jnp.minimum(i + 1, last), 0)
    nxt_bat = lambda i: (jnp.minimum(i + 1, last) // per_seq, 0, 0)
    out = jax.ShapeDtypeStruct((n, width), BF16)
    return pl.pallas_call(
        _att_proj_kernel,
        out_shape=(out, out, out, out),
        grid=(n // tm,),
        in_specs=[
            _resident((tm, d)),
            pl.BlockSpec((tm, d), nxt),
            _resident((1, d)),
            _resident((1, 1, d)),
            _resident((1, 1, d)),
            pl.BlockSpec((1, 1, d), nxt_bat),
            pl.BlockSpec((1, 1, d), nxt_bat),
            _resident(w_bf16.shape),
            _resident((1, ATT_HEAD_DIM)),
            _resident((1, ATT_HEAD_DIM)),
        ],
        out_specs=tuple(pl.BlockSpec((tm, width), row) for _ in range(4)),
        scratch_shapes=[pltpu.VMEM((tm, d), BF16), pltpu.VMEM((tm, d), BF16)],
        compiler_params=_params(("arbitrary",)),
        name="att_proj",
    )(x2, x2, norm_g, shift, scale, shift, scale, w_bf16, q_g, k_g)


def _attend(q, k, v_ext, bias, sg):
    s = lax.dot_general(q, k, (((1,), (1,)), ((), ())), preferred_element_type=F32) + bias
    m = jnp.max(s, axis=-1, keepdims=True)
    p = jnp.exp2(s - m).astype(BF16)
    o_ext = jnp.dot(p, v_ext, preferred_element_type=F32)
    o = o_ext[:, :ATT_HEAD_DIM] / o_ext[:, ATT_HEAD_DIM:]
    return (o * sg.astype(F32)).astype(BF16)


def _window_classes():
    chunks = ATT_QBLOCK // CHUNK
    far_chunks = LEFT_CHUNKS - MAX_REL // CHUNK
    table = []
    for ci in range(chunks):
        row = []
        for g in range(ATT_KWIN // 128):
            key_chunks = (2 * g, 2 * g + 1)
            valid = [ci <= ch <= ci + LEFT_CHUNKS for ch in key_chunks]
            far = [ci <= ch <= ci + far_chunks - 1 for ch in key_chunks]
            row.append("skip" if not any(valid) else "far" if all(far) else "near")
        table.append(row)
    return table


def _attend_full(q, k, v_ext, bias_ref, sg):
    s = lax.dot_general(q, k, (((1,), (1,)), ((), ())), preferred_element_type=F32)
    c_far = bias_ref[0:CHUNK, 0:128]
    p_rows = []
    for ci, classes in enumerate(_window_classes()):
        rows = slice(ci * CHUNK, (ci + 1) * CHUNK)
        logits = {}
        m_far = None
        m_near = None
        for g, cls in enumerate(classes):
            cols = slice(g * 128, (g + 1) * 128)
            if cls == "far":
                logits[g] = s[rows, cols]
                m_far = logits[g] if m_far is None else jnp.maximum(m_far, logits[g])
            elif cls == "near":
                logits[g] = s[rows, cols] + bias_ref[rows, cols]
                m_near = logits[g] if m_near is None else jnp.maximum(m_near, logits[g])
        m = jnp.max(jnp.maximum(m_far + c_far, m_near), axis=-1, keepdims=True)
        m = jnp.broadcast_to(m, (CHUNK, 128))
        m_minus_c = m - c_far
        parts = []
        for g, cls in enumerate(classes):
            if cls == "far":
                parts.append(jnp.exp2(logits[g] - m_minus_c).astype(BF16))
            elif cls == "near":
                parts.append(jnp.exp2(logits[g] - m).astype(BF16))
            else:
                parts.append(jnp.zeros((CHUNK, 128), BF16))
        p_rows.append(jnp.concatenate(parts, axis=1))
    p = jnp.concatenate(p_rows, axis=0)
    o_ext = jnp.dot(p, v_ext, preferred_element_type=F32)
    o = o_ext[:, :ATT_HEAD_DIM] / o_ext[:, ATT_HEAD_DIM:]
    return (o * sg.astype(F32)).astype(BF16)


def _attention_kernel(q_ref, k_ref, v_ref, sg_ref, rows_ref, o_ref, bias_ref):
    left = ATT_KWIN - ATT_QBLOCK
    n_blocks = q_ref.shape[0] // ATT_QBLOCK
    first_full = left // ATT_QBLOCK

    qi = lax.broadcasted_iota(jnp.int32, (ATT_QBLOCK, ATT_KWIN), 0)
    kj = lax.broadcasted_iota(jnp.int32, (ATT_QBLOCK, ATT_KWIN), 1)
    q_chunk = (qi + left) // CHUNK
    k_chunk = kj // CHUNK
    band = (k_chunk <= q_chunk) & (k_chunk >= q_chunk - LEFT_CHUNKS)
    for hh in range(2):
        tiled = jnp.broadcast_to(rows_ref[hh], (ATT_QBLOCK, ATT_BIAS_ROW))
        toeplitz = pltpu.roll(tiled, 0, 1, stride=1, stride_axis=0)
        bias_ref[hh] = jnp.where(band, toeplitz[:, :ATT_KWIN], NEG_INF)

    ones = jnp.ones((ATT_KWIN, ATT_HEAD_DIM), BF16)

    def run(row_start, key_start, n_keys):
        rows = pl.ds(row_start, ATT_QBLOCK)
        keys = pl.ds(key_start, n_keys)
        for hh in range(2):
            cols = slice(hh * ATT_HEAD_DIM, (hh + 1) * ATT_HEAD_DIM)
            v_ext = jnp.concatenate([v_ref[keys, cols], ones[:n_keys]], axis=1)
            if n_keys == ATT_KWIN:
                o_ref[rows, cols] = _attend_full(
                    q_ref[rows, cols], k_ref[keys, cols], v_ext, bias_ref.at[hh], sg_ref[rows, cols])
            else:
                o_ref[rows, cols] = _attend(
                    q_ref[rows, cols], k_ref[keys, cols], v_ext,
                    bias_ref[hh, :, ATT_KWIN - n_keys:], sg_ref[rows, cols])

    for blk in range(first_full):
        run(blk * ATT_QBLOCK, 0, (blk + 1) * ATT_QBLOCK)

    def body(blk, carry):
        row_start = pl.multiple_of(blk * ATT_QBLOCK, ATT_QBLOCK)
        run(row_start, pl.multiple_of(row_start - left, ATT_QBLOCK), ATT_KWIN)
        return carry

    lax.fori_loop(first_full, n_blocks, body, 0, unroll=ATT_UNROLL)


def _bias_rows(rel_table):
    left = ATT_KWIN - ATT_QBLOCK
    t = rel_table.astype(F32) * LOG2E
    h = t.shape[0]
    far = jnp.broadcast_to(t[:, -1:], (h, left - MAX_REL))
    near = jnp.broadcast_to(t[:, :1], (h, ATT_KWIN - (left - MAX_REL) - t.shape[1]))
    wrap = jnp.broadcast_to(t[:, -1:], (h, ATT_BIAS_ROW - ATT_KWIN))
    return jnp.concatenate([far, t[:, ::-1], near, wrap], axis=1).reshape(h, 1, ATT_BIAS_ROW)


def _attention(q, k, v, sg, rel_table, batch, seq):
    n, width = q.shape
    assert seq % ATT_QBLOCK == 0
    pair = 2 * ATT_HEAD_DIM
    seq_pair = pl.BlockSpec((seq, pair), lambda b, h: (b, h))
    return pl.pallas_call(
        _attention_kernel,
        out_shape=jax.ShapeDtypeStruct((n, width), BF16),
        grid=(batch, ATT_HEADS // 2),
        in_specs=[seq_pair, seq_pair, seq_pair, seq_pair,
                  pl.BlockSpec((2, 1, ATT_BIAS_ROW), lambda b, h: (h, 0, 0))],
        out_specs=seq_pair,
        scratch_shapes=[pltpu.VMEM((2, ATT_QBLOCK, ATT_KWIN), F32)],
        compiler_params=_params(("arbitrary", "arbitrary")),
        name="attention",
    )(q, k, v, sg, _bias_rows(rel_table))


def kernel(x, c, positions, norm_g, ada_w, ada_b, ret_w_in, ret_gn_g, ret_w_out,
           att_w_in, att_q_g, att_k_g, att_rel_bias, att_w_out):
    batch, seq, d = x.shape
    n = batch * seq
    mod = _adaln(c, ada_w, ada_b)
    shift = mod[:, :, :d].reshape(-1, batch, 1, d)
    scale = mod[:, :, d:2 * d].reshape(-1, batch, 1, d)
    gate = mod[:, :, 2 * d:].reshape(-1, batch, 1, d)

    x2 = x.reshape(n, d)
    pos2 = positions.reshape(n, 1)
    freq = (1.0 / (ROPE_BASE ** (jnp.arange(0, RET_QK_DIM, 2, dtype=F32) / RET_QK_DIM))).reshape(1, -1)

    proj0 = functools.partial(_ret_proj, x2, pos2, freq, norm_g[0].reshape(1, d), shift[0], scale[0],
                              ret_w_in[0].astype(BF16), seq)
    small = jnp.max(jnp.abs(positions)) < SMALL_ANGLE_LIMIT
    q, k, v, sg = lax.cond(small, lambda: proj0(True), lambda: proj0(False))
    o = _retention(q, k, v, sg, ret_gn_g[0], batch, seq)
    x2 = _out_proj(o, ret_w_out[0].astype(BF16), x2, gate[0], seq)

    q, k, v, sg = _att_proj(x2, norm_g[1].reshape(1, d), shift[1], scale[1],
                            att_w_in[0].astype(BF16), att_q_g[0].reshape(1, -1),
                            att_k_g[0].reshape(1, -1), seq)
    o = _attention(q, k, v, sg, att_rel_bias[0], batch, seq)
    x2 = _out_proj(o, att_w_out[0].astype(BF16), x2, gate[1], seq)
    return x2.reshape(batch, seq, d)
```

```python
import functools
import itertools
import math

import jax
import jax.numpy as jnp
from jax import lax
from jax.experimental import pallas as pl
from jax.experimental.pallas import tpu as pltpu

EPS = 1e-6
CHUNK = 64

RET_HEADS = 4
RET_QK_DIM = 256
RET_V_DIM = 512
ROPE_BASE = 10000.0

ATT_HEADS = 16
ATT_HEAD_DIM = 128
LEFT_CHUNKS = 8
MAX_REL = 2 * CHUNK
NEG_INF = -1e30

PROJ_ROWS = 512
OUT_PROJ_ROWS = 1024
RET_BLOCK = 256
RET_STEP_ROWS = 4096
RET_UNROLL = 4
ATT_QBLOCK = 256
ATT_UNROLL = 30
ATT_KWIN = ATT_QBLOCK + LEFT_CHUNKS * CHUNK
ATT_BIAS_ROW = 1024
LOG2E = math.log2(math.e)

VMEM_LIMIT_BYTES = 56 * 1024 * 1024

BF16 = jnp.bfloat16
F32 = jnp.float32


def _resident(shape):
    return pl.BlockSpec(shape, lambda *_: (0,) * len(shape), pipeline_mode=pl.Buffered(1))


def _params(semantics):
    return pltpu.CompilerParams(dimension_semantics=semantics, vmem_limit_bytes=VMEM_LIMIT_BYTES)


def _adaln_kernel(c_ref, w_ref, b_ref, o_ref):
    c = c_ref[...]
    cond = c * (1.0 / (1.0 + jnp.exp(-c)))
    o_ref[0] = jnp.dot(cond, w_ref[0], preferred_element_type=F32) + b_ref[0]


def _adaln(c, ada_w, ada_b):
    depth, d, d3 = ada_w.shape
    b = c.shape[0]
    col = 1024
    return pl.pallas_call(
        _adaln_kernel,
        out_shape=jax.ShapeDtypeStruct((depth, b, d3), F32),
        grid=(depth, d3 // col),
        in_specs=[
            pl.BlockSpec((b, d), lambda i, j: (0, 0)),
            pl.BlockSpec((1, d, col), lambda i, j: (i, 0, j)),
            pl.BlockSpec((1, 1, col), lambda i, j: (i, 0, j)),
        ],
        out_specs=pl.BlockSpec((1, b, col), lambda i, j: (i, 0, j)),
        compiler_params=_params(("arbitrary", "arbitrary")),
        name="adaln",
    )(c, ada_w, ada_b.reshape(depth, 1, d3))


PIECES_AFTER_BUSY_DOT = 1
PIECES_AFTER_CAST_DOT = 5
PROLOGUE_PIECES = 12 * PIECES_AFTER_BUSY_DOT + 4 * PIECES_AFTER_CAST_DOT


def _modulated_norm(x, g_ref, shift_ref, scale_ref):
    ms = jnp.mean(x * x, axis=-1, keepdims=True)
    h = x * lax.rsqrt(ms + EPS) * g_ref[...]
    h = h * (1.0 + scale_ref[0]) + shift_ref[0]
    return h.astype(BF16)


def _piece_rows(piece, tile_rows):
    n = tile_rows // PROLOGUE_PIECES
    return slice(piece * n, (piece + 1) * n)


def _silu(t):
    half = 0.5 * t
    return half + half * jnp.tanh(half)


def _zero_from(*tiles):
    acc = None
    for t in tiles:
        bits = pltpu.bitcast(t, jnp.uint32)
        for r in range(0, bits.shape[0], 8):
            for c in range(0, bits.shape[1], 128):
                part = bits[r:r + 8, c:c + 128]
                acc = part if acc is None else acc | part
    return ((acc >> 16) >> 16).astype(F32)


def _dot_then_pieces(h_ref, w_ref, col, width, pieces, count, next_piece):
    acc = jnp.dot(h_ref[...], w_ref[:, col:col + width], preferred_element_type=F32)
    for piece in itertools.islice(pieces, count):
        top = jnp.concatenate([acc[:8, :128] + next_piece(piece), acc[:8, 128:]], axis=1)
        acc = jnp.concatenate([top, acc[8:]], axis=0)
    return acc


HALF_PI_PARTS = (1.578125, -0.00732421875, -4.470348358154297e-06, 1.5893254712295857e-08)
SMALL_ANGLE_LIMIT = 2 ** 17
SIN_TAYLOR = tuple((-1.0) ** i / math.factorial(2 * i + 1) for i in range(8))
COS_TAYLOR = tuple((-1.0) ** i / math.factorial(2 * i) for i in range(8))


def _sincos_small(x):
    k = jnp.floor(x * (1.0 / math.pi) + 0.5)
    r = x
    for part in HALF_PI_PARTS:
        r = r - k * (2.0 * part)
    z = r * r
    s = SIN_TAYLOR[-1]
    c = COS_TAYLOR[-1]
    for s_coef, c_coef in zip(SIN_TAYLOR[-2::-1], COS_TAYLOR[-2::-1]):
        s = s * z + s_coef
        c = c * z + c_coef
    sign = 1.0 - 2.0 * (k - 2.0 * jnp.floor(0.5 * k))
    return sign * c, sign * (s * r)


def _ret_prologue(piece, x_ref, pos_ref, freq_ref, g_ref, shift_ref, scale_ref, h_ref, cos_ref, sin_ref,
                  small_angles):
    rows = _piece_rows(piece, x_ref.shape[0])
    h = _modulated_norm(x_ref[rows, :], g_ref, shift_ref, scale_ref)
    h_ref[rows, :] = h
    ang = pos_ref[rows, :].astype(F32) * freq_ref[...]
    if small_angles:
        cos, sin = _sincos_small(ang)
    else:
        cos = jnp.cos(ang)
        sin = jnp.sin(ang)
    cos_ref[rows, :] = cos
    sin_ref[rows, :] = sin
    return _zero_from(h, cos, sin)


def _ret_project(h_ref, cos_ref, sin_ref, w_ref, q_ref, k_ref, v_ref, sg_ref, next_piece):
    cos = cos_ref[...]
    sin = sin_ref[...]
    half = RET_QK_DIM // 2
    qk_width = RET_HEADS * RET_QK_DIM
    v_width = RET_HEADS * RET_V_DIM
    pieces = iter(range(PROLOGUE_PIECES))
    proj = functools.partial(_dot_then_pieces, h_ref, w_ref, pieces=pieces, next_piece=next_piece)

    for out_ref, base in ((q_ref, 0), (k_ref, qk_width)):
        for hh in range(RET_HEADS):
            acc = proj(base + hh * RET_QK_DIM, RET_QK_DIM, count=PIECES_AFTER_BUSY_DOT)
            t1 = acc[:, :half]
            t2 = acc[:, half:]
            out_ref[:, hh * RET_QK_DIM:hh * RET_QK_DIM + half] = (t1 * cos - t2 * sin).astype(BF16)
            out_ref[:, hh * RET_QK_DIM + half:(hh + 1) * RET_QK_DIM] = (t1 * sin + t2 * cos).astype(BF16)
    for c in range(0, v_width, RET_V_DIM):
        acc = proj(2 * qk_width + v_width + c, RET_V_DIM, count=PIECES_AFTER_BUSY_DOT)
        sg_ref[:, c:c + RET_V_DIM] = _silu(acc).astype(BF16)
    for c in range(0, v_width, RET_V_DIM):
        acc = proj(2 * qk_width + c, RET_V_DIM, count=PIECES_AFTER_CAST_DOT)
        v_ref[:, c:c + RET_V_DIM] = acc.astype(BF16)
    assert next(pieces, None) is None


def _ret_proj_kernel(x0_ref, xn_ref, pos0_ref, posn_ref, freq_ref, g_ref,
                     shift0_ref, scale0_ref, shiftn_ref, scalen_ref, w_ref,
                     q_ref, k_ref, v_ref, sg_ref,
                     h_a, cos_a, sin_a, h_b, cos_b, sin_b, *, small_angles):
    step = pl.program_id(0)

    @pl.when(step == 0)
    def _():
        for piece in range(PROLOGUE_PIECES):
            _ret_prologue(piece, x0_ref, pos0_ref, freq_ref, g_ref, shift0_ref, scale0_ref, h_a, cos_a, sin_a,
                          small_angles)

    def run(cur, nxt):
        next_piece = functools.partial(_ret_prologue, x_ref=xn_ref, pos_ref=posn_ref, freq_ref=freq_ref,
                                       g_ref=g_ref, shift_ref=shiftn_ref, scale_ref=scalen_ref,
                                       h_ref=nxt[0], cos_ref=nxt[1], sin_ref=nxt[2], small_angles=small_angles)
        _ret_project(*cur, w_ref, q_ref, k_ref, v_ref, sg_ref, next_piece)

    set_a = (h_a, cos_a, sin_a)
    set_b = (h_b, cos_b, sin_b)

    @pl.when(step % 2 == 0)
    def _():
        run(set_a, set_b)

    @pl.when(step % 2 == 1)
    def _():
        run(set_b, set_a)


def _ret_proj(x2, pos2, freq, norm_g, shift, scale, w_bf16, seq, small_angles):
    n, d = x2.shape
    tm = PROJ_ROWS
    assert n % tm == 0 and seq % tm == 0
    per_seq = seq // tm
    last = n // tm - 1
    qk_width = RET_HEADS * RET_QK_DIM
    v_width = RET_HEADS * RET_V_DIM
    half = RET_QK_DIM // 2
    row = lambda i: (i, 0)
    nxt = lambda i: (jnp.minimum(i + 1, last), 0)
    nxt_bat = lambda i: (jnp.minimum(i + 1, last) // per_seq, 0, 0)
    return pl.pallas_call(
        functools.partial(_ret_proj_kernel, small_angles=small_angles),
        out_shape=(
            jax.ShapeDtypeStruct((n, qk_width), BF16),
            jax.ShapeDtypeStruct((n, qk_width), BF16),
            jax.ShapeDtypeStruct((n, v_width), BF16),
            jax.ShapeDtypeStruct((n, v_width), BF16),
        ),
        grid=(n // tm,),
        in_specs=[
            _resident((tm, d)),
            pl.BlockSpec((tm, d), nxt),
            _resident((tm, 1)),
            pl.BlockSpec((tm, 1), nxt),
            _resident((1, half)),
            _resident((1, d)),
            _resident((1, 1, d)),
            _resident((1, 1, d)),
            pl.BlockSpec((1, 1, d), nxt_bat),
            pl.BlockSpec((1, 1, d), nxt_bat),
            _resident(w_bf16.shape),
        ],
        out_specs=(
            pl.BlockSpec((tm, qk_width), row),
            pl.BlockSpec((tm, qk_width), row),
            pl.BlockSpec((tm, v_width), row),
            pl.BlockSpec((tm, v_width), row),
        ),
        scratch_shapes=[pltpu.VMEM((tm, d), BF16), pltpu.VMEM((tm, half), F32), pltpu.VMEM((tm, half), F32),
                        pltpu.VMEM((tm, d), BF16), pltpu.VMEM((tm, half), F32), pltpu.VMEM((tm, half), F32)],
        compiler_params=_params(("arbitrary",)),
        name="ret_proj",
    )(x2, x2, pos2, pos2, freq, norm_g, shift, scale, shift, scale, w_bf16)


def _retention_kernel(q_ref, k_ref, v_ref, sg_ref, dmask_ref, qd_ref, kd_ref, bd_ref, gn_ref,
                      o_ref, state_ref):
    @pl.when(pl.program_id(2) == 0)
    def _():
        state_ref[...] = jnp.zeros_like(state_ref)

    dmask = dmask_ref[0]
    qd = qd_ref[0]
    kd = kd_ref[0]
    decay = bd_ref[0]
    gn = gn_ref[0]

    def body(blk, carry):
        rows = pl.ds(pl.multiple_of(blk * RET_BLOCK, RET_BLOCK), RET_BLOCK)
        q = q_ref[rows, :]
        k = k_ref[rows, :]
        v = v_ref[rows, :]
        state = state_ref[...]
        scores = lax.dot_general(q, k, (((1,), (1,)), ((), ())), preferred_element_type=F32)
        o = jnp.dot((scores * dmask).astype(BF16), v, preferred_element_type=F32)
        qs = q * qd
        o = o + jnp.dot(qs, state.astype(BF16), preferred_element_type=F32)
        ks = k * kd
        state_ref[...] = decay * state + lax.dot_general(
            ks, v, (((0,), (0,)), ((), ())), preferred_element_type=F32)
        ms = jnp.mean(o * o, axis=-1, keepdims=True)
        o = o * lax.rsqrt(ms + EPS) * gn
        o_ref[rows, :] = o.astype(BF16) * sg_ref[rows, :]
        return carry

    lax.fori_loop(0, RET_STEP_ROWS // RET_BLOCK, body, 0, unroll=RET_UNROLL)


def _retention_tables():
    log_gamma = jnp.log(1.0 - 2.0 ** (-5.0 - jnp.arange(RET_HEADS, dtype=F32)))
    t = jnp.arange(RET_BLOCK, dtype=F32)
    dist = t[:, None] - t[None, :]
    chunk = jnp.arange(RET_BLOCK) // CHUNK
    visible = chunk[None, :] <= chunk[:, None]
    k_scale = RET_QK_DIM ** -0.5
    dmask = jnp.where(visible[None], jnp.exp(log_gamma[:, None, None] * jnp.abs(dist)[None]), 0.0) * k_scale
    qd = jnp.exp(log_gamma[:, None] * (t + 1.0))
    kd = jnp.exp(log_gamma[:, None] * (RET_BLOCK - 1.0 - t)) * k_scale
    qd = jnp.broadcast_to(qd[:, :, None], (RET_HEADS, RET_BLOCK, RET_QK_DIM))
    kd = jnp.broadcast_to(kd[:, :, None], (RET_HEADS, RET_BLOCK, RET_QK_DIM))
    bd = jnp.broadcast_to(jnp.exp(log_gamma * RET_BLOCK)[:, None, None], (RET_HEADS, 1, RET_V_DIM))
    return dmask.astype(F32), qd.astype(BF16), kd.astype(BF16), bd.astype(F32)


def _retention(q, k, v, sg, gn_g, batch, seq):
    n = q.shape[0]
    rows = RET_STEP_ROWS
    assert seq % rows == 0
    per_seq = seq // rows
    dmask, qd, kd, bd = _retention_tables()
    row_head = lambda b, h, j: (b * per_seq + j, h)
    head_tab = lambda b, h, j: (h, 0, 0)
    return pl.pallas_call(
        _retention_kernel,
        out_shape=jax.ShapeDtypeStruct((n, RET_HEADS * RET_V_DIM), BF16),
        grid=(batch, RET_HEADS, per_seq),
        in_specs=[
            pl.BlockSpec((rows, RET_QK_DIM), row_head),
            pl.BlockSpec((rows, RET_QK_DIM), row_head),
            pl.BlockSpec((rows, RET_V_DIM), row_head),
            pl.BlockSpec((rows, RET_V_DIM), row_head),
            pl.BlockSpec((1, RET_BLOCK, RET_BLOCK), head_tab),
            pl.BlockSpec((1, RET_BLOCK, RET_QK_DIM), head_tab),
            pl.BlockSpec((1, RET_BLOCK, RET_QK_DIM), head_tab),
            pl.BlockSpec((1, 1, RET_V_DIM), head_tab),
            pl.BlockSpec((1, 1, RET_V_DIM), head_tab),
        ],
        out_specs=pl.BlockSpec((rows, RET_V_DIM), row_head),
        scratch_shapes=[pltpu.VMEM((RET_QK_DIM, RET_V_DIM), F32)],
        compiler_params=_params(("arbitrary", "arbitrary", "arbitrary")),
        name="retention",
    )(q, k, v, sg, dmask, qd, kd, bd, gn_g.reshape(RET_HEADS, 1, RET_V_DIM))


def _out_proj_kernel(o_ref, w_ref, x_ref, gate_ref, y_ref):
    out = jnp.dot(o_ref[...], w_ref[...], preferred_element_type=F32)
    y_ref[...] = x_ref[...] + gate_ref[0] * out


def _out_proj(o, w_bf16, x2, gate, seq):
    n, d = x2.shape
    width = o.shape[1]
    tm = OUT_PROJ_ROWS
    assert n % tm == 0 and seq % tm == 0
    per_seq = seq // tm
    row = lambda i: (i, 0)
    return pl.pallas_call(
        _out_proj_kernel,
        out_shape=jax.ShapeDtypeStruct((n, d), F32),
        grid=(n // tm,),
        in_specs=[
            pl.BlockSpec((tm, width), row),
            _resident(w_bf16.shape),
            pl.BlockSpec((tm, d), row),
            pl.BlockSpec((1, 1, d), lambda i: (i // per_seq, 0, 0)),
        ],
        out_specs=pl.BlockSpec((tm, d), row),
        compiler_params=_params(("arbitrary",)),
        name="out_proj",
    )(o, w_bf16, x2, gate)


def _att_prologue(piece, x_ref, g_ref, shift_ref, scale_ref, h_ref):
    rows = _piece_rows(piece, x_ref.shape[0])
    h = _modulated_norm(x_ref[rows, :], g_ref, shift_ref, scale_ref)
    h_ref[rows, :] = h
    return _zero_from(h)


def _att_project(h_ref, w_ref, qg_ref, kg_ref, q_ref, k_ref, v_ref, sg_ref, next_piece):
    width = ATT_HEADS * ATT_HEAD_DIM
    chunk = 512
    pieces = iter(range(PROLOGUE_PIECES))
    proj = functools.partial(_dot_then_pieces, h_ref, w_ref, pieces=pieces, next_piece=next_piece)

    for out_ref, gain_ref, base, post in ((q_ref, qg_ref, 0, ATT_HEAD_DIM ** -0.5 * LOG2E), (k_ref, kg_ref, width, 1.0)):
        gain = gain_ref[...] * post
        for c in range(0, width, chunk):
            acc = proj(base + c, chunk, count=PIECES_AFTER_BUSY_DOT)
            for hh in range(chunk // ATT_HEAD_DIM):
                a = acc[:, hh * ATT_HEAD_DIM:(hh + 1) * ATT_HEAD_DIM]
                ms = jnp.mean(a * a, axis=-1, keepdims=True)
                out_ref[:, c + hh * ATT_HEAD_DIM:c + (hh + 1) * ATT_HEAD_DIM] = (
                    a * lax.rsqrt(ms + EPS) * gain).astype(BF16)
    for c in range(0, width, chunk):
        sg_ref[:, c:c + chunk] = _silu(proj(3 * width + c, chunk, count=PIECES_AFTER_BUSY_DOT)).astype(BF16)
    for c in range(0, width, chunk):
        v_ref[:, c:c + chunk] = proj(2 * width + c, chunk, count=PIECES_AFTER_CAST_DOT).astype(BF16)
    assert next(pieces, None) is None


def _att_proj_kernel(x0_ref, xn_ref, g_ref, shift0_ref, scale0_ref, shiftn_ref, scalen_ref,
                     w_ref, qg_ref, kg_ref, q_ref, k_ref, v_ref, sg_ref, h_a, h_b):
    step = pl.program_id(0)

    @pl.when(step == 0)
    def _():
        for piece in range(PROLOGUE_PIECES):
            _att_prologue(piece, x0_ref, g_ref, shift0_ref, scale0_ref, h_a)

    def run(cur, nxt):
        next_piece = functools.partial(_att_prologue, x_ref=xn_ref, g_ref=g_ref, shift_ref=shiftn_ref,
                                       scale_ref=scalen_ref, h_ref=nxt)
        _att_project(cur, w_ref, qg_ref, kg_ref, q_ref, k_ref, v_ref, sg_ref, next_piece)

    @pl.when(step % 2 == 0)
    def _():
        run(h_a, h_b)

    @pl.when(step % 2 == 1)
    def _():
        run(h_b, h_a)


def _att_proj(x2, norm_g, shift, scale, w_bf16, q_g, k_g, seq):
    n, d = x2.shape
    tm = PROJ_ROWS
    per_seq = seq // tm
    last = n // tm - 1
    width = ATT_HEADS * ATT_HEAD_DIM
    row = lambda i: (i, 0)
    nxt = lambda i: (jnp.minimum(i + 1, last), 0)
    nxt_bat = lambda i: (jnp.minimum(i + 1, last) // per_seq, 0, 0)
    out = jax.ShapeDtypeStruct((n, width), BF16)
    return pl.pallas_call(
        _att_proj_kernel,
        out_shape=(out, out, out, out),
        grid=(n // tm,),
        in_specs=[
            _resident((tm, d)),
            pl.BlockSpec((tm, d), nxt),
            _resident((1, d)),
            _resident((1, 1, d)),
            _resident((1, 1, d)),
            pl.BlockSpec((1, 1, d), nxt_bat),
            pl.BlockSpec((1, 1, d), nxt_bat),
            _resident(w_bf16.shape),
            _resident((1, ATT_HEAD_DIM)),
            _resident((1, ATT_HEAD_DIM)),
        ],
        out_specs=tuple(pl.BlockSpec((tm, width), row) for _ in range(4)),
        scratch_shapes=[pltpu.VMEM((tm, d), BF16), pltpu.VMEM((tm, d), BF16)],
        compiler_params=_params(("arbitrary",)),
        name="att_proj",
    )(x2, x2, norm_g, shift, scale, shift, scale, w_bf16, q_g, k_g)


def _attend(q, k, v_ext, bias, sg):
    s = lax.dot_general(q, k, (((1,), (1,)), ((), ())), preferred_element_type=F32) + bias
    m = jnp.max(s, axis=-1, keepdims=True)
    p = jnp.exp2(s - m).astype(BF16)
    o_ext = jnp.dot(p, v_ext, preferred_element_type=F32)
    o = o_ext[:, :ATT_HEAD_DIM] / o_ext[:, ATT_HEAD_DIM:]
    return (o * sg.astype(F32)).astype(BF16)


def _window_classes():
    chunks = ATT_QBLOCK // CHUNK
    far_chunks = LEFT_CHUNKS - MAX_REL // CHUNK
    table = []
    for ci in range(chunks):
        row = []
        for g in range(ATT_KWIN // 128):
            key_chunks = (2 * g, 2 * g + 1)
            valid = [ci <= ch <= ci + LEFT_CHUNKS for ch in key_chunks]
            far = [ci <= ch <= ci + far_chunks - 1 for ch in key_chunks]
            row.append("skip" if not any(valid) else "far" if all(far) else "near")
        table.append(row)
    return table


def _attend_full(q, k, v_ext, bias_ref, sg):
    s = lax.dot_general(q, k, (((1,), (1,)), ((), ())), preferred_element_type=F32)
    c_far = bias_ref[0:CHUNK, 0:128]
    p_rows = []
    for ci, classes in enumerate(_window_classes()):
        rows = slice(ci * CHUNK, (ci + 1) * CHUNK)
        logits = {}
        m_far = None
        m_near = None
        for g, cls in enumerate(classes):
            cols = slice(g * 128, (g + 1) * 128)
            if cls == "far":
                logits[g] = s[rows, cols]
                m_far = logits[g] if m_far is None else jnp.maximum(m_far, logits[g])
            elif cls == "near":
                logits[g] = s[rows, cols] + bias_ref[rows, cols]
                m_near = logits[g] if m_near is None else jnp.maximum(m_near, logits[g])
        m = jnp.max(jnp.maximum(m_far + c_far, m_near), axis=-1, keepdims=True)
        m = jnp.broadcast_to(m, (CHUNK, 128))
        m_minus_c = m - c_far
        parts = []
        for g, cls in enumerate(classes):
            if cls == "far":
                parts.append(jnp.exp2(logits[g] - m_minus_c).astype(BF16))
            elif cls == "near":
                parts.append(jnp.exp2(logits[g] - m).astype(BF16))
            else:
                parts.append(jnp.zeros((CHUNK, 128), BF16))
        p_rows.append(jnp.concatenate(parts, axis=1))
    p = jnp.concatenate(p_rows, axis=0)
    o_ext = jnp.dot(p, v_ext, preferred_element_type=F32)
    o = o_ext[:, :ATT_HEAD_DIM] / o_ext[:, ATT_HEAD_DIM:]
    return (o * sg.astype(F32)).astype(BF16)


def _attention_kernel(q_ref, k_ref, v_ref, sg_ref, rows_ref, o_ref, bias_ref):
    left = ATT_KWIN - ATT_QBLOCK
    n_blocks = q_ref.shape[0] // ATT_QBLOCK
    first_full = left // ATT_QBLOCK

    @pl.when(pl.program_id(1) == 0)
    def _():
        qi = lax.broadcasted_iota(jnp.int32, (ATT_QBLOCK, ATT_KWIN), 0)
        kj = lax.broadcasted_iota(jnp.int32, (ATT_QBLOCK, ATT_KWIN), 1)
        q_chunk = (qi + left) // CHUNK
        k_chunk = kj // CHUNK
        band = (k_chunk <= q_chunk) & (k_chunk >= q_chunk - LEFT_CHUNKS)
        for hh in range(2):
            tiled = jnp.broadcast_to(rows_ref[hh], (ATT_QBLOCK, ATT_BIAS_ROW))
            toeplitz = pltpu.roll(tiled, 0, 1, stride=1, stride_axis=0)
            bias_ref[hh] = jnp.where(band, toeplitz[:, :ATT_KWIN], NEG_INF)

    ones = jnp.ones((ATT_KWIN, ATT_HEAD_DIM), BF16)

    def run(row_start, key_start, n_keys):
        rows = pl.ds(row_start, ATT_QBLOCK)
        keys = pl.ds(key_start, n_keys)
        for hh in range(2):
            cols = slice(hh * ATT_HEAD_DIM, (hh + 1) * ATT_HEAD_DIM)
            v_ext = jnp.concatenate([v_ref[keys, cols], ones[:n_keys]], axis=1)
            if n_keys == ATT_KWIN:
                o_ref[rows, cols] = _attend_full(
                    q_ref[rows, cols], k_ref[keys, cols], v_ext, bias_ref.at[hh], sg_ref[rows, cols])
            else:
                o_ref[rows, cols] = _attend(
                    q_ref[rows, cols], k_ref[keys, cols], v_ext,
                    bias_ref[hh, :, ATT_KWIN - n_keys:], sg_ref[rows, cols])

    for blk in range(first_full):
        run(blk * ATT_QBLOCK, 0, (blk + 1) * ATT_QBLOCK)

    def body(blk, carry):
        row_start = pl.multiple_of(blk * ATT_QBLOCK, ATT_QBLOCK)
        run(row_start, pl.multiple_of(row_start - left, ATT_QBLOCK), ATT_KWIN)
        return carry

    lax.fori_loop(first_full, n_blocks, body, 0, unroll=ATT_UNROLL)


def _bias_rows(rel_table):
    left = ATT_KWIN - ATT_QBLOCK
    t = rel_table.astype(F32) * LOG2E
    h = t.shape[0]
    far = jnp.broadcast_to(t[:, -1:], (h, left - MAX_REL))
    near = jnp.broadcast_to(t[:, :1], (h, ATT_KWIN - (left - MAX_REL) - t.shape[1]))
    wrap = jnp.broadcast_to(t[:, -1:], (h, ATT_BIAS_ROW - ATT_KWIN))
    return jnp.concatenate([far, t[:, ::-1], near, wrap], axis=1).reshape(h, 1, ATT_BIAS_ROW)


def _attention(q, k, v, sg, rel_table, batch, seq):
    n, width = q.shape
    assert seq % ATT_QBLOCK == 0
    pair = 2 * ATT_HEAD_DIM
    seq_pair = pl.BlockSpec((seq, pair), lambda h, b: (b, h))
    return pl.pallas_call(
        _attention_kernel,
        out_shape=jax.ShapeDtypeStruct((n, width), BF16),
        grid=(ATT_HEADS // 2, batch),
        in_specs=[seq_pair, seq_pair, seq_pair, seq_pair,
                  pl.BlockSpec((2, 1, ATT_BIAS_ROW), lambda h, b: (h, 0, 0))],
        out_specs=seq_pair,
        scratch_shapes=[pltpu.VMEM((2, ATT_QBLOCK, ATT_KWIN), F32)],
        compiler_params=_params(("arbitrary", "arbitrary")),
        name="attention",
    )(q, k, v, sg, _bias_rows(rel_table))


def kernel(x, c, positions, norm_g, ada_w, ada_b, ret_w_in, ret_gn_g, ret_w_out,
           att_w_in, att_q_g, att_k_g, att_rel_bias, att_w_out):
    batch, seq, d = x.shape
    n = batch * seq
    mod = _adaln(c, ada_w, ada_b)
    shift = mod[:, :, :d].reshape(-1, batch, 1, d)
    scale = mod[:, :, d:2 * d].reshape(-1, batch, 1, d)
    gate = mod[:, :, 2 * d:].reshape(-1, batch, 1, d)

    x2 = x.reshape(n, d)
    pos2 = positions.reshape(n, 1)
    freq = (1.0 / (ROPE_BASE ** (jnp.arange(0, RET_QK_DIM, 2, dtype=F32) / RET_QK_DIM))).reshape(1, -1)

    proj0 = functools.partial(_ret_proj, x2, pos2, freq, norm_g[0].reshape(1, d), shift[0], scale[0],
                              ret_w_in[0].astype(BF16), seq)
    small = jnp.max(jnp.abs(positions)) < SMALL_ANGLE_LIMIT
    q, k, v, sg = lax.cond(small, lambda: proj0(True), lambda: proj0(False))
    o = _retention(q, k, v, sg, ret_gn_g[0], batch, seq)
    x2 = _out_proj(o, ret_w_out[0].astype(BF16), x2, gate[0], seq)

    q, k, v, sg = _att_proj(x2, norm_g[1].reshape(1, d), shift[1], scale[1],
                            att_w_in[0].astype(BF16), att_q_g[0].reshape(1, -1),
                            att_k_g[0].reshape(1, -1), seq)
    o = _attention(q, k, v, sg, att_rel_bias[0], batch, seq)
    x2 = _out_proj(o, att_w_out[0].astype(BF16), x2, gate[1], seq)
    return x2.reshape(batch, seq, d)
```

```python
import functools
import itertools
import math

import jax
import jax.numpy as jnp
from jax import lax
from jax.experimental import pallas as pl
from jax.experimental.pallas import tpu as pltpu

EPS = 1e-6
CHUNK = 64

RET_HEADS = 4
RET_QK_DIM = 256
RET_V_DIM = 512
ROPE_BASE = 10000.0

ATT_HEADS = 16
ATT_HEAD_DIM = 128
LEFT_CHUNKS = 8
MAX_REL = 2 * CHUNK
NEG_INF = -1e30

PROJ_ROWS = 512
OUT_PROJ_ROWS = 1024
RET_BLOCK = 256
RET_STEP_ROWS = 1024
ATT_QBLOCK = 256
ATT_UNROLL = 30
ATT_KWIN = ATT_QBLOCK + LEFT_CHUNKS * CHUNK
ATT_BIAS_ROW = 1024
LOG2E = math.log2(math.e)

VMEM_LIMIT_BYTES = 56 * 1024 * 1024

BF16 = jnp.bfloat16
F32 = jnp.float32


def _resident(shape):
    return pl.BlockSpec(shape, lambda *_: (0,) * len(shape), pipeline_mode=pl.Buffered(1))


def _params(semantics):
    return pltpu.CompilerParams(dimension_semantics=semantics, vmem_limit_bytes=VMEM_LIMIT_BYTES)


def _adaln_kernel(c_ref, w_ref, b_ref, o_ref):
    c = c_ref[...]
    cond = c * (1.0 / (1.0 + jnp.exp(-c)))
    o_ref[0] = jnp.dot(cond, w_ref[0], preferred_element_type=F32) + b_ref[0]


def _adaln(c, ada_w, ada_b):
    depth, d, d3 = ada_w.shape
    b = c.shape[0]
    col = 1024
    return pl.pallas_call(
        _adaln_kernel,
        out_shape=jax.ShapeDtypeStruct((depth, b, d3), F32),
        grid=(depth, d3 // col),
        in_specs=[
            pl.BlockSpec((b, d), lambda i, j: (0, 0)),
            pl.BlockSpec((1, d, col), lambda i, j: (i, 0, j)),
            pl.BlockSpec((1, 1, col), lambda i, j: (i, 0, j)),
        ],
        out_specs=pl.BlockSpec((1, b, col), lambda i, j: (i, 0, j)),
        compiler_params=_params(("arbitrary", "arbitrary")),
        name="adaln",
    )(c, ada_w, ada_b.reshape(depth, 1, d3))


PIECES_AFTER_BUSY_DOT = 1
PIECES_AFTER_CAST_DOT = 5
PROLOGUE_PIECES = 12 * PIECES_AFTER_BUSY_DOT + 4 * PIECES_AFTER_CAST_DOT


def _modulated_norm(x, g_ref, shift_ref, scale_ref):
    ms = jnp.mean(x * x, axis=-1, keepdims=True)
    h = x * lax.rsqrt(ms + EPS) * g_ref[...]
    h = h * (1.0 + scale_ref[0]) + shift_ref[0]
    return h.astype(BF16)


def _piece_rows(piece, tile_rows):
    n = tile_rows // PROLOGUE_PIECES
    return slice(piece * n, (piece + 1) * n)


def _silu(t):
    half = 0.5 * t
    return half + half * jnp.tanh(half)


def _zero_from(*tiles):
    acc = None
    for t in tiles:
        bits = pltpu.bitcast(t, jnp.uint32)
        for r in range(0, bits.shape[0], 8):
            for c in range(0, bits.shape[1], 128):
                part = bits[r:r + 8, c:c + 128]
                acc = part if acc is None else acc | part
    return ((acc >> 16) >> 16).astype(F32)


def _dot_then_pieces(h_ref, w_ref, col, width, pieces, count, next_piece):
    acc = jnp.dot(h_ref[...], w_ref[:, col:col + width], preferred_element_type=F32)
    for piece in itertools.islice(pieces, count):
        top = jnp.concatenate([acc[:8, :128] + next_piece(piece), acc[:8, 128:]], axis=1)
        acc = jnp.concatenate([top, acc[8:]], axis=0)
    return acc


HALF_PI_PARTS = (1.578125, -0.00732421875, -4.470348358154297e-06, 1.5893254712295857e-08)
SMALL_ANGLE_LIMIT = 2 ** 17
SIN_TAYLOR = tuple((-1.0) ** i / math.factorial(2 * i + 1) for i in range(8))
COS_TAYLOR = tuple((-1.0) ** i / math.factorial(2 * i) for i in range(8))


def _sincos_small(x):
    k = jnp.floor(x * (1.0 / math.pi) + 0.5)
    r = x
    for part in HALF_PI_PARTS:
        r = r - k * (2.0 * part)
    z = r * r
    s = SIN_TAYLOR[-1]
    c = COS_TAYLOR[-1]
    for s_coef, c_coef in zip(SIN_TAYLOR[-2::-1], COS_TAYLOR[-2::-1]):
        s = s * z + s_coef
        c = c * z + c_coef
    sign = 1.0 - 2.0 * (k - 2.0 * jnp.floor(0.5 * k))
    return sign * c, sign * (s * r)


def _ret_prologue(piece, x_ref, pos_ref, freq_ref, g_ref, shift_ref, scale_ref, h_ref, cos_ref, sin_ref,
                  small_angles):
    rows = _piece_rows(piece, x_ref.shape[0])
    h = _modulated_norm(x_ref[rows, :], g_ref, shift_ref, scale_ref)
    h_ref[rows, :] = h
    ang = pos_ref[rows, :].astype(F32) * freq_ref[...]
    if small_angles:
        cos, sin = _sincos_small(ang)
    else:
        cos = jnp.cos(ang)
        sin = jnp.sin(ang)
    cos_ref[rows, :] = cos
    sin_ref[rows, :] = sin
    return _zero_from(h, cos, sin)


def _ret_project(h_ref, cos_ref, sin_ref, w_ref, q_ref, k_ref, v_ref, sg_ref, next_piece):
    cos = cos_ref[...]
    sin = sin_ref[...]
    half = RET_QK_DIM // 2
    qk_width = RET_HEADS * RET_QK_DIM
    v_width = RET_HEADS * RET_V_DIM
    pieces = iter(range(PROLOGUE_PIECES))
    proj = functools.partial(_dot_then_pieces, h_ref, w_ref, pieces=pieces, next_piece=next_piece)

    for out_ref, base in ((q_ref, 0), (k_ref, qk_width)):
        for hh in range(RET_HEADS):
            acc = proj(base + hh * RET_QK_DIM, RET_QK_DIM, count=PIECES_AFTER_BUSY_DOT)
            t1 = acc[:, :half]
            t2 = acc[:, half:]
            out_ref[:, hh * RET_QK_DIM:hh * RET_QK_DIM + half] = (t1 * cos - t2 * sin).astype(BF16)
            out_ref[:, hh * RET_QK_DIM + half:(hh + 1) * RET_QK_DIM] = (t1 * sin + t2 * cos).astype(BF16)
    for c in range(0, v_width, RET_V_DIM):
        acc = proj(2 * qk_width + v_width + c, RET_V_DIM, count=PIECES_AFTER_BUSY_DOT)
        sg_ref[:, c:c + RET_V_DIM] = _silu(acc).astype(BF16)
    for c in range(0, v_width, RET_V_DIM):
        acc = proj(2 * qk_width + c, RET_V_DIM, count=PIECES_AFTER_CAST_DOT)
        v_ref[:, c:c + RET_V_DIM] = acc.astype(BF16)
    assert next(pieces, None) is None


def _ret_proj_kernel(x0_ref, xn_ref, pos0_ref, posn_ref, freq_ref, g_ref,
                     shift0_ref, scale0_ref, shiftn_ref, scalen_ref, w_ref,
                     q_ref, k_ref, v_ref, sg_ref,
                     h_a, cos_a, sin_a, h_b, cos_b, sin_b, *, small_angles):
    step = pl.program_id(0)

    @pl.when(step == 0)
    def _():
        for piece in range(PROLOGUE_PIECES):
            _ret_prologue(piece, x0_ref, pos0_ref, freq_ref, g_ref, shift0_ref, scale0_ref, h_a, cos_a, sin_a,
                          small_angles)

    def run(cur, nxt):
        next_piece = functools.partial(_ret_prologue, x_ref=xn_ref, pos_ref=posn_ref, freq_ref=freq_ref,
                                       g_ref=g_ref, shift_ref=shiftn_ref, scale_ref=scalen_ref,
                                       h_ref=nxt[0], cos_ref=nxt[1], sin_ref=nxt[2], small_angles=small_angles)
        _ret_project(*cur, w_ref, q_ref, k_ref, v_ref, sg_ref, next_piece)

    set_a = (h_a, cos_a, sin_a)
    set_b = (h_b, cos_b, sin_b)

    @pl.when(step % 2 == 0)
    def _():
        run(set_a, set_b)

    @pl.when(step % 2 == 1)
    def _():
        run(set_b, set_a)


def _ret_proj(x2, pos2, freq, norm_g, shift, scale, w_bf16, seq, small_angles):
    n, d = x2.shape
    tm = PROJ_ROWS
    assert n % tm == 0 and seq % tm == 0
    per_seq = seq // tm
    last = n // tm - 1
    qk_width = RET_HEADS * RET_QK_DIM
    v_width = RET_HEADS * RET_V_DIM
    half = RET_QK_DIM // 2
    row = lambda i: (i, 0)
    nxt = lambda i: (jnp.minimum(i + 1, last), 0)
    nxt_bat = lambda i: (jnp.minimum(i + 1, last) // per_seq, 0, 0)
    return pl.pallas_call(
        functools.partial(_ret_proj_kernel, small_angles=small_angles),
        out_shape=(
            jax.ShapeDtypeStruct((n, qk_width), BF16),
            jax.ShapeDtypeStruct((n, qk_width), BF16),
            jax.ShapeDtypeStruct((n, v_width), BF16),
            jax.ShapeDtypeStruct((n, v_width), BF16),
        ),
        grid=(n // tm,),
        in_specs=[
            _resident((tm, d)),
            pl.BlockSpec((tm, d), nxt),
            _resident((tm, 1)),
            pl.BlockSpec((tm, 1), nxt),
            _resident((1, half)),
            _resident((1, d)),
            _resident((1, 1, d)),
            _resident((1, 1, d)),
            pl.BlockSpec((1, 1, d), nxt_bat),
            pl.BlockSpec((1, 1, d), nxt_bat),
            _resident(w_bf16.shape),
        ],
        out_specs=(
            pl.BlockSpec((tm, qk_width), row),
            pl.BlockSpec((tm, qk_width), row),
            pl.BlockSpec((tm, v_width), row),
            pl.BlockSpec((tm, v_width), row),
        ),
        scratch_shapes=[pltpu.VMEM((tm, d), BF16), pltpu.VMEM((tm, half), F32), pltpu.VMEM((tm, half), F32),
                        pltpu.VMEM((tm, d), BF16), pltpu.VMEM((tm, half), F32), pltpu.VMEM((tm, half), F32)],
        compiler_params=_params(("arbitrary",)),
        name="ret_proj",
    )(x2, x2, pos2, pos2, freq, norm_g, shift, scale, shift, scale, w_bf16)


def _retention_kernel(q_ref, k_ref, v_ref, sg_ref, dmask_ref, qd_ref, kd_ref, bd_ref, gn_ref,
                      o_ref, state_ref):
    @pl.when(pl.program_id(1) == 0)
    def _():
        state_ref[...] = jnp.zeros_like(state_ref)

    for blk in range(RET_STEP_ROWS // RET_BLOCK):
        rows = slice(blk * RET_BLOCK, (blk + 1) * RET_BLOCK)
        for head in range(RET_HEADS):
            qk_cols = slice(head * RET_QK_DIM, (head + 1) * RET_QK_DIM)
            v_cols = slice(head * RET_V_DIM, (head + 1) * RET_V_DIM)
            q = q_ref[rows, qk_cols]
            k = k_ref[rows, qk_cols]
            v = v_ref[rows, v_cols]
            state = state_ref[head]
            scores = lax.dot_general(q, k, (((1,), (1,)), ((), ())), preferred_element_type=F32)
            o = jnp.dot((scores * dmask_ref[head]).astype(BF16), v, preferred_element_type=F32)
            qs = q * qd_ref[head]
            o = o + jnp.dot(qs, state.astype(BF16), preferred_element_type=F32)
            ks = k * kd_ref[head]
            state_ref[head] = bd_ref[head] * state + lax.dot_general(
                ks, v, (((0,), (0,)), ((), ())), preferred_element_type=F32)
            ms = jnp.mean(o * o, axis=-1, keepdims=True)
            o = o * lax.rsqrt(ms + EPS) * gn_ref[head]
            o_ref[rows, v_cols] = o.astype(BF16) * sg_ref[rows, v_cols]


def _retention_tables():
    log_gamma = jnp.log(1.0 - 2.0 ** (-5.0 - jnp.arange(RET_HEADS, dtype=F32)))
    t = jnp.arange(RET_BLOCK, dtype=F32)
    dist = t[:, None] - t[None, :]
    chunk = jnp.arange(RET_BLOCK) // CHUNK
    visible = chunk[None, :] <= chunk[:, None]
    k_scale = RET_QK_DIM ** -0.5
    dmask = jnp.where(visible[None], jnp.exp(log_gamma[:, None, None] * jnp.abs(dist)[None]), 0.0) * k_scale
    qd = jnp.exp(log_gamma[:, None] * (t + 1.0))
    kd = jnp.exp(log_gamma[:, None] * (RET_BLOCK - 1.0 - t)) * k_scale
    qd = jnp.broadcast_to(qd[:, :, None], (RET_HEADS, RET_BLOCK, RET_QK_DIM))
    kd = jnp.broadcast_to(kd[:, :, None], (RET_HEADS, RET_BLOCK, RET_QK_DIM))
    bd = jnp.broadcast_to(jnp.exp(log_gamma * RET_BLOCK)[:, None, None], (RET_HEADS, 1, RET_V_DIM))
    return dmask.astype(F32), qd.astype(BF16), kd.astype(BF16), bd.astype(F32)


def _retention(q, k, v, sg, gn_g, batch, seq):
    n = q.shape[0]
    rows = RET_STEP_ROWS
    assert seq % rows == 0
    per_seq = seq // rows
    dmask, qd, kd, bd = _retention_tables()
    qk_width = RET_HEADS * RET_QK_DIM
    v_width = RET_HEADS * RET_V_DIM
    row = lambda b, j: (b * per_seq + j, 0)
    return pl.pallas_call(
        _retention_kernel,
        out_shape=jax.ShapeDtypeStruct((n, v_width), BF16),
        grid=(batch, per_seq),
        in_specs=[
            pl.BlockSpec((rows, qk_width), row),
            pl.BlockSpec((rows, qk_width), row),
            pl.BlockSpec((rows, v_width), row),
            pl.BlockSpec((rows, v_width), row),
            _resident(dmask.shape),
            _resident(qd.shape),
            _resident(kd.shape),
            _resident(bd.shape),
            _resident((RET_HEADS, 1, RET_V_DIM)),
        ],
        out_specs=pl.BlockSpec((rows, v_width), row),
        scratch_shapes=[pltpu.VMEM((RET_HEADS, RET_QK_DIM, RET_V_DIM), F32)],
        compiler_params=_params(("arbitrary", "arbitrary")),
        name="retention",
    )(q, k, v, sg, dmask, qd, kd, bd, gn_g.reshape(RET_HEADS, 1, RET_V_DIM))


def _out_proj_kernel(o_ref, w_ref, x_ref, gate_ref, y_ref):
    out = jnp.dot(o_ref[...], w_ref[...], preferred_element_type=F32)
    y_ref[...] = x_ref[...] + gate_ref[0] * out


def _out_proj(o, w_bf16, x2, gate, seq):
    n, d = x2.shape
    width = o.shape[1]
    tm = OUT_PROJ_ROWS
    assert n % tm == 0 and seq % tm == 0
    per_seq = seq // tm
    row = lambda i: (i, 0)
    return pl.pallas_call(
        _out_proj_kernel,
        out_shape=jax.ShapeDtypeStruct((n, d), F32),
        grid=(n // tm,),
        in_specs=[
            pl.BlockSpec((tm, width), row),
            _resident(w_bf16.shape),
            pl.BlockSpec((tm, d), row),
            pl.BlockSpec((1, 1, d), lambda i: (i // per_seq, 0, 0)),
        ],
        out_specs=pl.BlockSpec((tm, d), row),
        compiler_params=_params(("arbitrary",)),
        name="out_proj",
    )(o, w_bf16, x2, gate)


def _att_prologue(piece, x_ref, g_ref, shift_ref, scale_ref, h_ref):
    rows = _piece_rows(piece, x_ref.shape[0])
    h = _modulated_norm(x_ref[rows, :], g_ref, shift_ref, scale_ref)
    h_ref[rows, :] = h
    return _zero_from(h)


def _att_project(h_ref, w_ref, qg_ref, kg_ref, q_ref, k_ref, v_ref, sg_ref, next_piece):
    width = ATT_HEADS * ATT_HEAD_DIM
    chunk = 512
    pieces = iter(range(PROLOGUE_PIECES))
    proj = functools.partial(_dot_then_pieces, h_ref, w_ref, pieces=pieces, next_piece=next_piece)

    for out_ref, gain_ref, base, post in ((q_ref, qg_ref, 0, ATT_HEAD_DIM ** -0.5 * LOG2E), (k_ref, kg_ref, width, 1.0)):
        gain = gain_ref[...] * post
        for c in range(0, width, chunk):
            acc = proj(base + c, chunk, count=PIECES_AFTER_BUSY_DOT)
            for hh in range(chunk // ATT_HEAD_DIM):
                a = acc[:, hh * ATT_HEAD_DIM:(hh + 1) * ATT_HEAD_DIM]
                ms = jnp.mean(a * a, axis=-1, keepdims=True)
                out_ref[:, c + hh * ATT_HEAD_DIM:c + (hh + 1) * ATT_HEAD_DIM] = (
                    a * lax.rsqrt(ms + EPS) * gain).astype(BF16)
    for c in range(0, width, chunk):
        sg_ref[:, c:c + chunk] = _silu(proj(3 * width + c, chunk, count=PIECES_AFTER_BUSY_DOT)).astype(BF16)
    for c in range(0, width, chunk):
        v_ref[:, c:c + chunk] = proj(2 * width + c, chunk, count=PIECES_AFTER_CAST_DOT).astype(BF16)
    assert next(pieces, None) is None


def _att_proj_kernel(x0_ref, xn_ref, g_ref, shift0_ref, scale0_ref, shiftn_ref, scalen_ref,
                     w_ref, qg_ref, kg_ref, q_ref, k_ref, v_ref, sg_ref, h_a, h_b):
    step = pl.program_id(0)

    @pl.when(step == 0)
    def _():
        for piece in range(PROLOGUE_PIECES):
            _att_prologue(piece, x0_ref, g_ref, shift0_ref, scale0_ref, h_a)

    def run(cur, nxt):
        next_piece = functools.partial(_att_prologue, x_ref=xn_ref, g_ref=g_ref, shift_ref=shiftn_ref,
                                       scale_ref=scalen_ref, h_ref=nxt)
        _att_project(cur, w_ref, qg_ref, kg_ref, q_ref, k_ref, v_ref, sg_ref, next_piece)

    @pl.when(step % 2 == 0)
    def _():
        run(h_a, h_b)

    @pl.when(step % 2 == 1)
    def _():
        run(h_b, h_a)


def _att_proj(x2, norm_g, shift, scale, w_bf16, q_g, k_g, seq):
    n, d = x2.shape
    tm = PROJ_ROWS
    per_seq = seq // tm
    last = n // tm - 1
    width = ATT_HEADS * ATT_HEAD_DIM
    row = lambda i: (i, 0)
    nxt = lambda i: (jnp.minimum(i + 1, last), 0)
    nxt_bat = lambda i: (jnp.minimum(i + 1, last) // per_seq, 0, 0)
    out = jax.ShapeDtypeStruct((n, width), BF16)
    return pl.pallas_call(
        _att_proj_kernel,
        out_shape=(out, out, out, out),
        grid=(n // tm,),
        in_specs=[
            _resident((tm, d)),
            pl.BlockSpec((tm, d), nxt),
            _resident((1, d)),
            _resident((1, 1, d)),
            _resident((1, 1, d)),
            pl.BlockSpec((1, 1, d), nxt_bat),
            pl.BlockSpec((1, 1, d), nxt_bat),
            _resident(w_bf16.shape),
            _resident((1, ATT_HEAD_DIM)),
            _resident((1, ATT_HEAD_DIM)),
        ],
        out_specs=tuple(pl.BlockSpec((tm, width), row) for _ in range(4)),
        scratch_shapes=[pltpu.VMEM((tm, d), BF16), pltpu.VMEM((tm, d), BF16)],
        compiler_params=_params(("arbitrary",)),
        name="att_proj",
    )(x2, x2, norm_g, shift, scale, shift, scale, w_bf16, q_g, k_g)


def _attend(q, k, v_ext, bias, sg):
    s = lax.dot_general(q, k, (((1,), (1,)), ((), ())), preferred_element_type=F32) + bias
    m = jnp.max(s, axis=-1, keepdims=True)
    p = jnp.exp2(s - m).astype(BF16)
    o_ext = jnp.dot(p, v_ext, preferred_element_type=F32)
    o = o_ext[:, :ATT_HEAD_DIM] / o_ext[:, ATT_HEAD_DIM:]
    return (o * sg.astype(F32)).astype(BF16)


def _window_classes():
    chunks = ATT_QBLOCK // CHUNK
    far_chunks = LEFT_CHUNKS - MAX_REL // CHUNK
    table = []
    for ci in range(chunks):
        row = []
        for g in range(ATT_KWIN // 128):
            key_chunks = (2 * g, 2 * g + 1)
            valid = [ci <= ch <= ci + LEFT_CHUNKS for ch in key_chunks]
            far = [ci <= ch <= ci + far_chunks - 1 for ch in key_chunks]
            row.append("skip" if not any(valid) else "far" if all(far) else "near")
        table.append(row)
    return table


def _attend_full(q, k, v_ext, bias_ref, sg):
    s = lax.dot_general(q, k, (((1,), (1,)), ((), ())), preferred_element_type=F32)
    c_far = bias_ref[0:CHUNK, 0:128]
    p_rows = []
    for ci, classes in enumerate(_window_classes()):
        rows = slice(ci * CHUNK, (ci + 1) * CHUNK)
        logits = {}
        m_far = None
        m_near = None
        for g, cls in enumerate(classes):
            cols = slice(g * 128, (g + 1) * 128)
            if cls == "far":
                logits[g] = s[rows, cols]
                m_far = logits[g] if m_far is None else jnp.maximum(m_far, logits[g])
            elif cls == "near":
                logits[g] = s[rows, cols] + bias_ref[rows, cols]
                m_near = logits[g] if m_near is None else jnp.maximum(m_near, logits[g])
        m = jnp.max(jnp.maximum(m_far + c_far, m_near), axis=-1, keepdims=True)
        m = jnp.broadcast_to(m, (CHUNK, 128))
        m_minus_c = m - c_far
        parts = []
        for g, cls in enumerate(classes):
            if cls == "far":
                parts.append(jnp.exp2(logits[g] - m_minus_c).astype(BF16))
            elif cls == "near":
                parts.append(jnp.exp2(logits[g] - m).astype(BF16))
            else:
                parts.append(jnp.zeros((CHUNK, 128), BF16))
        p_rows.append(jnp.concatenate(parts, axis=1))
    p = jnp.concatenate(p_rows, axis=0)
    o_ext = jnp.dot(p, v_ext, preferred_element_type=F32)
    o = o_ext[:, :ATT_HEAD_DIM] / o_ext[:, ATT_HEAD_DIM:]
    return (o * sg.astype(F32)).astype(BF16)


def _attention_kernel(q_ref, k_ref, v_ref, sg_ref, rows_ref, o_ref, bias_ref):
    left = ATT_KWIN - ATT_QBLOCK
    n_blocks = q_ref.shape[0] // ATT_QBLOCK
    first_full = left // ATT_QBLOCK

    @pl.when(pl.program_id(1) == 0)
    def _():
        qi = lax.broadcasted_iota(jnp.int32, (ATT_QBLOCK, ATT_KWIN), 0)
        kj = lax.broadcasted_iota(jnp.int32, (ATT_QBLOCK, ATT_KWIN), 1)
        q_chunk = (qi + left) // CHUNK
        k_chunk = kj // CHUNK
        band = (k_chunk <= q_chunk) & (k_chunk >= q_chunk - LEFT_CHUNKS)
        for hh in range(2):
            tiled = jnp.broadcast_to(rows_ref[hh], (ATT_QBLOCK, ATT_BIAS_ROW))
            toeplitz = pltpu.roll(tiled, 0, 1, stride=1, stride_axis=0)
            bias_ref[hh] = jnp.where(band, toeplitz[:, :ATT_KWIN], NEG_INF)

    ones = jnp.ones((ATT_KWIN, ATT_HEAD_DIM), BF16)

    def run(row_start, key_start, n_keys):
        rows = pl.ds(row_start, ATT_QBLOCK)
        keys = pl.ds(key_start, n_keys)
        for hh in range(2):
            cols = slice(hh * ATT_HEAD_DIM, (hh + 1) * ATT_HEAD_DIM)
            v_ext = jnp.concatenate([v_ref[keys, cols], ones[:n_keys]], axis=1)
            if n_keys == ATT_KWIN:
                o_ref[rows, cols] = _attend_full(
                    q_ref[rows, cols], k_ref[keys, cols], v_ext, bias_ref.at[hh], sg_ref[rows, cols])
            else:
                o_ref[rows, cols] = _attend(
                    q_ref[rows, cols], k_ref[keys, cols], v_ext,
                    bias_ref[hh, :, ATT_KWIN - n_keys:], sg_ref[rows, cols])

    for blk in range(first_full):
        run(blk * ATT_QBLOCK, 0, (blk + 1) * ATT_QBLOCK)

    def body(blk, carry):
        row_start = pl.multiple_of(blk * ATT_QBLOCK, ATT_QBLOCK)
        run(row_start, pl.multiple_of(row_start - left, ATT_QBLOCK), ATT_KWIN)
        return carry

    lax.fori_loop(first_full, n_blocks, body, 0, unroll=ATT_UNROLL)


def _bias_rows(rel_table):
    left = ATT_KWIN - ATT_QBLOCK
    t = rel_table.astype(F32) * LOG2E
    h = t.shape[0]
    far = jnp.broadcast_to(t[:, -1:], (h, left - MAX_REL))
    near = jnp.broadcast_to(t[:, :1], (h, ATT_KWIN - (left - MAX_REL) - t.shape[1]))
    wrap = jnp.broadcast_to(t[:, -1:], (h, ATT_BIAS_ROW - ATT_KWIN))
    return jnp.concatenate([far, t[:, ::-1], near, wrap], axis=1).reshape(h, 1, ATT_BIAS_ROW)


def _attention(q, k, v, sg, rel_table, batch, seq):
    n, width = q.shape
    assert seq % ATT_QBLOCK == 0
    pair = 2 * ATT_HEAD_DIM
    seq_pair = pl.BlockSpec((seq, pair), lambda h, b: (b, h))
    return pl.pallas_call(
        _attention_kernel,
        out_shape=jax.ShapeDtypeStruct((n, width), BF16),
        grid=(ATT_HEADS // 2, batch),
        in_specs=[seq_pair, seq_pair, seq_pair, seq_pair,
                  pl.BlockSpec((2, 1, ATT_BIAS_ROW), lambda h, b: (h, 0, 0))],
        out_specs=seq_pair,
        scratch_shapes=[pltpu.VMEM((2, ATT_QBLOCK, ATT_KWIN), F32)],
        compiler_params=_params(("arbitrary", "arbitrary")),
        name="attention",
    )(q, k, v, sg, _bias_rows(rel_table))


def kernel(x, c, positions, norm_g, ada_w, ada_b, ret_w_in, ret_gn_g, ret_w_out,
           att_w_in, att_q_g, att_k_g, att_rel_bias, att_w_out):
    batch, seq, d = x.shape
    n = batch * seq
    mod = _adaln(c, ada_w, ada_b)
    shift = mod[:, :, :d].reshape(-1, batch, 1, d)
    scale = mod[:, :, d:2 * d].reshape(-1, batch, 1, d)
    gate = mod[:, :, 2 * d:].reshape(-1, batch, 1, d)

    x2 = x.reshape(n, d)
    pos2 = positions.reshape(n, 1)
    freq = (1.0 / (ROPE_BASE ** (jnp.arange(0, RET_QK_DIM, 2, dtype=F32) / RET_QK_DIM))).reshape(1, -1)

    proj0 = functools.partial(_ret_proj, x2, pos2, freq, norm_g[0].reshape(1, d), shift[0], scale[0],
                              ret_w_in[0].astype(BF16), seq)
    small = jnp.max(jnp.abs(positions)) < SMALL_ANGLE_LIMIT
    q, k, v, sg = lax.cond(small, lambda: proj0(True), lambda: proj0(False))
    o = _retention(q, k, v, sg, ret_gn_g[0], batch, seq)
    x2 = _out_proj(o, ret_w_out[0].astype(BF16), x2, gate[0], seq)

    q, k, v, sg = _att_proj(x2, norm_g[1].reshape(1, d), shift[1], scale[1],
                            att_w_in[0].astype(BF16), att_q_g[0].reshape(1, -1),
                            att_k_g[0].reshape(1, -1), seq)
    o = _attention(q, k, v, sg, att_rel_bias[0], batch, seq)
    x2 = _out_proj(o, att_w_out[0].astype(BF16), x2, gate[1], seq)
    return x2.reshape(batch, seq, d)
```

```python
import functools
import itertools
import math

import jax
import jax.numpy as jnp
from jax import lax
from jax.experimental import pallas as pl
from jax.experimental.pallas import tpu as pltpu

EPS = 1e-6
CHUNK = 64

RET_HEADS = 4
RET_QK_DIM = 256
RET_V_DIM = 512
ROPE_BASE = 10000.0

ATT_HEADS = 16
ATT_HEAD_DIM = 128
LEFT_CHUNKS = 8
MAX_REL = 2 * CHUNK
NEG_INF = -1e30

PROJ_ROWS = 512
OUT_PROJ_ROWS = 1024
RET_BLOCK = 256
RET_STEP_ROWS = 1024
ATT_QBLOCK = 256
ATT_UNROLL = 30
ATT_KWIN = ATT_QBLOCK + LEFT_CHUNKS * CHUNK
ATT_BIAS_ROW = 1024
LOG2E = math.log2(math.e)

VMEM_LIMIT_BYTES = 56 * 1024 * 1024

BF16 = jnp.bfloat16
F32 = jnp.float32


def _resident(shape):
    return pl.BlockSpec(shape, lambda *_: (0,) * len(shape), pipeline_mode=pl.Buffered(1))


def _params(semantics):
    return pltpu.CompilerParams(dimension_semantics=semantics, vmem_limit_bytes=VMEM_LIMIT_BYTES)


def _adaln_kernel(c_ref, w_ref, b_ref, o_ref):
    c = c_ref[...]
    cond = c * (1.0 / (1.0 + jnp.exp(-c)))
    o_ref[0] = jnp.dot(cond, w_ref[0], preferred_element_type=F32) + b_ref[0]


def _adaln(c, ada_w, ada_b):
    depth, d, d3 = ada_w.shape
    b = c.shape[0]
    col = 1024
    return pl.pallas_call(
        _adaln_kernel,
        out_shape=jax.ShapeDtypeStruct((depth, b, d3), F32),
        grid=(depth, d3 // col),
        in_specs=[
            pl.BlockSpec((b, d), lambda i, j: (0, 0)),
            pl.BlockSpec((1, d, col), lambda i, j: (i, 0, j)),
            pl.BlockSpec((1, 1, col), lambda i, j: (i, 0, j)),
        ],
        out_specs=pl.BlockSpec((1, b, col), lambda i, j: (i, 0, j)),
        compiler_params=_params(("arbitrary", "arbitrary")),
        name="adaln",
    )(c, ada_w, ada_b.reshape(depth, 1, d3))


PIECES_AFTER_BUSY_DOT = 1
PIECES_AFTER_CAST_DOT = 5
PROLOGUE_PIECES = 12 * PIECES_AFTER_BUSY_DOT + 4 * PIECES_AFTER_CAST_DOT


def _modulated_norm(x, g_ref, shift_ref, scale_ref):
    ms = jnp.mean(x * x, axis=-1, keepdims=True)
    h = x * lax.rsqrt(ms + EPS) * g_ref[...]
    h = h * (1.0 + scale_ref[0]) + shift_ref[0]
    return h.astype(BF16)


def _piece_rows(piece, tile_rows):
    n = tile_rows // PROLOGUE_PIECES
    return slice(piece * n, (piece + 1) * n)


def _silu(t):
    half = 0.5 * t
    return half + half * jnp.tanh(half)


def _zero_from(*tiles):
    acc = None
    for t in tiles:
        bits = pltpu.bitcast(t, jnp.uint32)
        for r in range(0, bits.shape[0], 8):
            for c in range(0, bits.shape[1], 128):
                part = bits[r:r + 8, c:c + 128]
                acc = part if acc is None else acc | part
    return ((acc >> 16) >> 16).astype(F32)


def _dot_then_pieces(h_ref, w_ref, col, width, pieces, count, next_piece):
    acc = jnp.dot(h_ref[...], w_ref[:, col:col + width], preferred_element_type=F32)
    for piece in itertools.islice(pieces, count):
        top = jnp.concatenate([acc[:8, :128] + next_piece(piece), acc[:8, 128:]], axis=1)
        acc = jnp.concatenate([top, acc[8:]], axis=0)
    return acc


HALF_PI_PARTS = (1.578125, -0.00732421875, -4.470348358154297e-06, 1.5893254712295857e-08)
SMALL_ANGLE_LIMIT = 2 ** 17
SIN_TAYLOR = tuple((-1.0) ** i / math.factorial(2 * i + 1) for i in range(8))
COS_TAYLOR = tuple((-1.0) ** i / math.factorial(2 * i) for i in range(8))


def _sincos_small(x):
    k = jnp.floor(x * (1.0 / math.pi) + 0.5)
    r = x
    for part in HALF_PI_PARTS:
        r = r - k * (2.0 * part)
    z = r * r
    s = SIN_TAYLOR[-1]
    c = COS_TAYLOR[-1]
    for s_coef, c_coef in zip(SIN_TAYLOR[-2::-1], COS_TAYLOR[-2::-1]):
        s = s * z + s_coef
        c = c * z + c_coef
    sign = 1.0 - 2.0 * (k - 2.0 * jnp.floor(0.5 * k))
    return sign * c, sign * (s * r)


def _ret_prologue(piece, x_ref, pos_ref, freq_ref, g_ref, shift_ref, scale_ref, h_ref, cos_ref, sin_ref,
                  small_angles):
    rows = _piece_rows(piece, x_ref.shape[0])
    h = _modulated_norm(x_ref[rows, :], g_ref, shift_ref, scale_ref)
    h_ref[rows, :] = h
    ang = pos_ref[rows, :].astype(F32) * freq_ref[...]
    if small_angles:
        cos, sin = _sincos_small(ang)
    else:
        cos = jnp.cos(ang)
        sin = jnp.sin(ang)
    cos_ref[rows, :] = cos
    sin_ref[rows, :] = sin
    return _zero_from(h, cos, sin)


def _ret_project(h_ref, cos_ref, sin_ref, w_ref, q_ref, k_ref, v_ref, sg_ref, next_piece):
    cos = cos_ref[...]
    sin = sin_ref[...]
    half = RET_QK_DIM // 2
    qk_width = RET_HEADS * RET_QK_DIM
    v_width = RET_HEADS * RET_V_DIM
    pieces = iter(range(PROLOGUE_PIECES))
    proj = functools.partial(_dot_then_pieces, h_ref, w_ref, pieces=pieces, next_piece=next_piece)

    for out_ref, base in ((q_ref, 0), (k_ref, qk_width)):
        for hh in range(RET_HEADS):
            acc = proj(base + hh * RET_QK_DIM, RET_QK_DIM, count=PIECES_AFTER_BUSY_DOT)
            t1 = acc[:, :half]
            t2 = acc[:, half:]
            out_ref[:, hh * RET_QK_DIM:hh * RET_QK_DIM + half] = (t1 * cos - t2 * sin).astype(BF16)
            out_ref[:, hh * RET_QK_DIM + half:(hh + 1) * RET_QK_DIM] = (t1 * sin + t2 * cos).astype(BF16)
    for c in range(0, v_width, RET_V_DIM):
        acc = proj(2 * qk_width + v_width + c, RET_V_DIM, count=PIECES_AFTER_BUSY_DOT)
        sg_ref[:, c:c + RET_V_DIM] = _silu(acc).astype(BF16)
    for c in range(0, v_width, RET_V_DIM):
        acc = proj(2 * qk_width + c, RET_V_DIM, count=PIECES_AFTER_CAST_DOT)
        v_ref[:, c:c + RET_V_DIM] = acc.astype(BF16)
    assert next(pieces, None) is None


def _ret_proj_kernel(x0_ref, xn_ref, pos0_ref, posn_ref, freq_ref, g_ref,
                     shift0_ref, scale0_ref, shiftn_ref, scalen_ref, w_ref,
                     q_ref, k_ref, v_ref, sg_ref,
                     h_a, cos_a, sin_a, h_b, cos_b, sin_b, *, small_angles):
    step = pl.program_id(0)

    @pl.when(step == 0)
    def _():
        for piece in range(PROLOGUE_PIECES):
            _ret_prologue(piece, x0_ref, pos0_ref, freq_ref, g_ref, shift0_ref, scale0_ref, h_a, cos_a, sin_a,
                          small_angles)

    def run(cur, nxt):
        next_piece = functools.partial(_ret_prologue, x_ref=xn_ref, pos_ref=posn_ref, freq_ref=freq_ref,
                                       g_ref=g_ref, shift_ref=shiftn_ref, scale_ref=scalen_ref,
                                       h_ref=nxt[0], cos_ref=nxt[1], sin_ref=nxt[2], small_angles=small_angles)
        _ret_project(*cur, w_ref, q_ref, k_ref, v_ref, sg_ref, next_piece)

    set_a = (h_a, cos_a, sin_a)
    set_b = (h_b, cos_b, sin_b)

    @pl.when(step % 2 == 0)
    def _():
        run(set_a, set_b)

    @pl.when(step % 2 == 1)
    def _():
        run(set_b, set_a)


def _ret_proj(x2, pos2, freq, norm_g, shift, scale, w_bf16, seq, small_angles):
    n, d = x2.shape
    tm = PROJ_ROWS
    assert n % tm == 0 and seq % tm == 0
    per_seq = seq // tm
    last = n // tm - 1
    qk_width = RET_HEADS * RET_QK_DIM
    v_width = RET_HEADS * RET_V_DIM
    half = RET_QK_DIM // 2
    row = lambda i: (i, 0)
    nxt = lambda i: (jnp.minimum(i + 1, last), 0)
    nxt_bat = lambda i: (jnp.minimum(i + 1, last) // per_seq, 0, 0)
    return pl.pallas_call(
        functools.partial(_ret_proj_kernel, small_angles=small_angles),
        out_shape=(
            jax.ShapeDtypeStruct((n, qk_width), BF16),
            jax.ShapeDtypeStruct((n, qk_width), BF16),
            jax.ShapeDtypeStruct((n, v_width), BF16),
            jax.ShapeDtypeStruct((n, v_width), BF16),
        ),
        grid=(n // tm,),
        in_specs=[
            _resident((tm, d)),
            pl.BlockSpec((tm, d), nxt),
            _resident((tm, 1)),
            pl.BlockSpec((tm, 1), nxt),
            _resident((1, half)),
            _resident((1, d)),
            _resident((1, 1, d)),
            _resident((1, 1, d)),
            pl.BlockSpec((1, 1, d), nxt_bat),
            pl.BlockSpec((1, 1, d), nxt_bat),
            _resident(w_bf16.shape),
        ],
        out_specs=(
            pl.BlockSpec((tm, qk_width), row),
            pl.BlockSpec((tm, qk_width), row),
            pl.BlockSpec((tm, v_width), row),
            pl.BlockSpec((tm, v_width), row),
        ),
        scratch_shapes=[pltpu.VMEM((tm, d), BF16), pltpu.VMEM((tm, half), F32), pltpu.VMEM((tm, half), F32),
                        pltpu.VMEM((tm, d), BF16), pltpu.VMEM((tm, half), F32), pltpu.VMEM((tm, half), F32)],
        compiler_params=_params(("arbitrary",)),
        name="ret_proj",
    )(x2, x2, pos2, pos2, freq, norm_g, shift, scale, shift, scale, w_bf16)


def _retention_kernel(q_ref, k_ref, v_ref, sg_ref, dmask_ref, qd_ref, kd_ref, bd_ref, gn_ref,
                      w_ref, x_ref, gate_ref, y_ref, state_ref):
    @pl.when(pl.program_id(1) == 0)
    def _():
        state_ref[...] = jnp.zeros_like(state_ref)

    for blk in range(RET_STEP_ROWS // RET_BLOCK):
        rows = slice(blk * RET_BLOCK, (blk + 1) * RET_BLOCK)
        gated = []
        for head in range(RET_HEADS):
            qk_cols = slice(head * RET_QK_DIM, (head + 1) * RET_QK_DIM)
            v_cols = slice(head * RET_V_DIM, (head + 1) * RET_V_DIM)
            q = q_ref[rows, qk_cols]
            k = k_ref[rows, qk_cols]
            v = v_ref[rows, v_cols]
            state = state_ref[head]
            scores = lax.dot_general(q, k, (((1,), (1,)), ((), ())), preferred_element_type=F32)
            o = jnp.dot((scores * dmask_ref[head]).astype(BF16), v, preferred_element_type=F32)
            qs = q * qd_ref[head]
            o = o + jnp.dot(qs, state.astype(BF16), preferred_element_type=F32)
            ks = k * kd_ref[head]
            state_ref[head] = bd_ref[head] * state + lax.dot_general(
                ks, v, (((0,), (0,)), ((), ())), preferred_element_type=F32)
            ms = jnp.mean(o * o, axis=-1, keepdims=True)
            o = o * lax.rsqrt(ms + EPS) * gn_ref[head]
            gated.append(o.astype(BF16) * sg_ref[rows, v_cols])
        out = jnp.dot(jnp.concatenate(gated, axis=1), w_ref[...], preferred_element_type=F32)
        y_ref[rows, :] = x_ref[rows, :] + gate_ref[0] * out


def _retention_tables():
    log_gamma = jnp.log(1.0 - 2.0 ** (-5.0 - jnp.arange(RET_HEADS, dtype=F32)))
    t = jnp.arange(RET_BLOCK, dtype=F32)
    dist = t[:, None] - t[None, :]
    chunk = jnp.arange(RET_BLOCK) // CHUNK
    visible = chunk[None, :] <= chunk[:, None]
    k_scale = RET_QK_DIM ** -0.5
    dmask = jnp.where(visible[None], jnp.exp(log_gamma[:, None, None] * jnp.abs(dist)[None]), 0.0) * k_scale
    qd = jnp.exp(log_gamma[:, None] * (t + 1.0))
    kd = jnp.exp(log_gamma[:, None] * (RET_BLOCK - 1.0 - t)) * k_scale
    qd = jnp.broadcast_to(qd[:, :, None], (RET_HEADS, RET_BLOCK, RET_QK_DIM))
    kd = jnp.broadcast_to(kd[:, :, None], (RET_HEADS, RET_BLOCK, RET_QK_DIM))
    bd = jnp.broadcast_to(jnp.exp(log_gamma * RET_BLOCK)[:, None, None], (RET_HEADS, 1, RET_V_DIM))
    return dmask.astype(F32), qd.astype(BF16), kd.astype(BF16), bd.astype(F32)


def _retention(q, k, v, sg, gn_g, w_out_bf16, x2, gate, batch, seq):
    n, d = x2.shape
    rows = RET_STEP_ROWS
    assert seq % rows == 0
    per_seq = seq // rows
    dmask, qd, kd, bd = _retention_tables()
    qk_width = RET_HEADS * RET_QK_DIM
    v_width = RET_HEADS * RET_V_DIM
    row = lambda b, j: (b * per_seq + j, 0)
    return pl.pallas_call(
        _retention_kernel,
        out_shape=jax.ShapeDtypeStruct((n, d), F32),
        grid=(batch, per_seq),
        in_specs=[
            pl.BlockSpec((rows, qk_width), row),
            pl.BlockSpec((rows, qk_width), row),
            pl.BlockSpec((rows, v_width), row),
            pl.BlockSpec((rows, v_width), row),
            _resident(dmask.shape),
            _resident(qd.shape),
            _resident(kd.shape),
            _resident(bd.shape),
            _resident((RET_HEADS, 1, RET_V_DIM)),
            _resident(w_out_bf16.shape),
            pl.BlockSpec((rows, d), row),
            pl.BlockSpec((1, 1, d), lambda b, j: (b, 0, 0)),
        ],
        out_specs=pl.BlockSpec((rows, d), row),
        scratch_shapes=[pltpu.VMEM((RET_HEADS, RET_QK_DIM, RET_V_DIM), F32)],
        compiler_params=_params(("arbitrary", "arbitrary")),
        name="retention",
    )(q, k, v, sg, dmask, qd, kd, bd, gn_g.reshape(RET_HEADS, 1, RET_V_DIM), w_out_bf16, x2, gate)


def _out_proj_kernel(o_ref, w_ref, x_ref, gate_ref, y_ref):
    out = jnp.dot(o_ref[...], w_ref[...], preferred_element_type=F32)
    y_ref[...] = x_ref[...] + gate_ref[0] * out


def _out_proj(o, w_bf16, x2, gate, seq):
    n, d = x2.shape
    width = o.shape[1]
    tm = OUT_PROJ_ROWS
    assert n % tm == 0 and seq % tm == 0
    per_seq = seq // tm
    row = lambda i: (i, 0)
    return pl.pallas_call(
        _out_proj_kernel,
        out_shape=jax.ShapeDtypeStruct((n, d), F32),
        grid=(n // tm,),
        in_specs=[
            pl.BlockSpec((tm, width), row),
            _resident(w_bf16.shape),
            pl.BlockSpec((tm, d), row),
            pl.BlockSpec((1, 1, d), lambda i: (i // per_seq, 0, 0)),
        ],
        out_specs=pl.BlockSpec((tm, d), row),
        compiler_params=_params(("arbitrary",)),
        name="out_proj",
    )(o, w_bf16, x2, gate)


def _att_prologue(piece, x_ref, g_ref, shift_ref, scale_ref, h_ref):
    rows = _piece_rows(piece, x_ref.shape[0])
    h = _modulated_norm(x_ref[rows, :], g_ref, shift_ref, scale_ref)
    h_ref[rows, :] = h
    return _zero_from(h)


def _att_project(h_ref, w_ref, qg_ref, kg_ref, q_ref, k_ref, v_ref, sg_ref, next_piece):
    width = ATT_HEADS * ATT_HEAD_DIM
    chunk = 512
    pieces = iter(range(PROLOGUE_PIECES))
    proj = functools.partial(_dot_then_pieces, h_ref, w_ref, pieces=pieces, next_piece=next_piece)

    for out_ref, gain_ref, base, post in ((q_ref, qg_ref, 0, ATT_HEAD_DIM ** -0.5 * LOG2E), (k_ref, kg_ref, width, 1.0)):
        gain = gain_ref[...] * post
        for c in range(0, width, chunk):
            acc = proj(base + c, chunk, count=PIECES_AFTER_BUSY_DOT)
            for hh in range(chunk // ATT_HEAD_DIM):
                a = acc[:, hh * ATT_HEAD_DIM:(hh + 1) * ATT_HEAD_DIM]
                ms = jnp.mean(a * a, axis=-1, keepdims=True)
                out_ref[:, c + hh * ATT_HEAD_DIM:c + (hh + 1) * ATT_HEAD_DIM] = (
                    a * lax.rsqrt(ms + EPS) * gain).astype(BF16)
    for c in range(0, width, chunk):
        sg_ref[:, c:c + chunk] = _silu(proj(3 * width + c, chunk, count=PIECES_AFTER_BUSY_DOT)).astype(BF16)
    for c in range(0, width, chunk):
        v_ref[:, c:c + chunk] = proj(2 * width + c, chunk, count=PIECES_AFTER_CAST_DOT).astype(BF16)
    assert next(pieces, None) is None


def _att_proj_kernel(x0_ref, xn_ref, g_ref, shift0_ref, scale0_ref, shiftn_ref, scalen_ref,
                     w_ref, qg_ref, kg_ref, q_ref, k_ref, v_ref, sg_ref, h_a, h_b):
    step = pl.program_id(0)

    @pl.when(step == 0)
    def _():
        for piece in range(PROLOGUE_PIECES):
            _att_prologue(piece, x0_ref, g_ref, shift0_ref, scale0_ref, h_a)

    def run(cur, nxt):
        next_piece = functools.partial(_att_prologue, x_ref=xn_ref, g_ref=g_ref, shift_ref=shiftn_ref,
                                       scale_ref=scalen_ref, h_ref=nxt)
        _att_project(cur, w_ref, qg_ref, kg_ref, q_ref, k_ref, v_ref, sg_ref, next_piece)

    @pl.when(step % 2 == 0)
    def _():
        run(h_a, h_b)

    @pl.when(step % 2 == 1)
    def _():
        run(h_b, h_a)


def _att_proj(x2, norm_g, shift, scale, w_bf16, q_g, k_g, seq):
    n, d = x2.shape
    tm = PROJ_ROWS
    per_seq = seq // tm
    last = n // tm - 1
    width = ATT_HEADS * ATT_HEAD_DIM
    row = lambda i: (i, 0)
    nxt = lambda i: (jnp.minimum(i + 1, last), 0)
    nxt_bat = lambda i: (jnp.minimum(i + 1, last) // per_seq, 0, 0)
    out = jax.ShapeDtypeStruct((n, width), BF16)
    return pl.pallas_call(
        _att_proj_kernel,
        out_shape=(out, out, out, out),
        grid=(n // tm,),
        in_specs=[
            _resident((tm, d)),
            pl.BlockSpec((tm, d), nxt),
            _resident((1, d)),
            _resident((1, 1, d)),
            _resident((1, 1, d)),
            pl.BlockSpec((1, 1, d), nxt_bat),
            pl.BlockSpec((1, 1, d), nxt_bat),
            _resident(w_bf16.shape),
            _resident((1, ATT_HEAD_DIM)),
            _resident((1, ATT_HEAD_DIM)),
        ],
        out_specs=tuple(pl.BlockSpec((tm, width), row) for _ in range(4)),
        scratch_shapes=[pltpu.VMEM((tm, d), BF16), pltpu.VMEM((tm, d), BF16)],
        compiler_params=_params(("arbitrary",)),
        name="att_proj",
    )(x2, x2, norm_g, shift, scale, shift, scale, w_bf16, q_g, k_g)


def _attend(q, k, v_ext, bias, sg):
    s = lax.dot_general(q, k, (((1,), (1,)), ((), ())), preferred_element_type=F32) + bias
    m = jnp.max(s, axis=-1, keepdims=True)
    p = jnp.exp2(s - m).astype(BF16)
    o_ext = jnp.dot(p, v_ext, preferred_element_type=F32)
    o = o_ext[:, :ATT_HEAD_DIM] / o_ext[:, ATT_HEAD_DIM:]
    return (o * sg.astype(F32)).astype(BF16)


def _window_classes():
    chunks = ATT_QBLOCK // CHUNK
    far_chunks = LEFT_CHUNKS - MAX_REL // CHUNK
    table = []
    for ci in range(chunks):
        row = []
        for g in range(ATT_KWIN // 128):
            key_chunks = (2 * g, 2 * g + 1)
            valid = [ci <= ch <= ci + LEFT_CHUNKS for ch in key_chunks]
            far = [ci <= ch <= ci + far_chunks - 1 for ch in key_chunks]
            row.append("skip" if not any(valid) else "far" if all(far) else "near")
        table.append(row)
    return table


def _attend_full(q, k, v_ext, bias_ref, sg):
    s = lax.dot_general(q, k, (((1,), (1,)), ((), ())), preferred_element_type=F32)
    c_far = bias_ref[0:CHUNK, 0:128]
    p_rows = []
    for ci, classes in enumerate(_window_classes()):
        rows = slice(ci * CHUNK, (ci + 1) * CHUNK)
        logits = {}
        m_far = None
        m_near = None
        for g, cls in enumerate(classes):
            cols = slice(g * 128, (g + 1) * 128)
            if cls == "far":
                logits[g] = s[rows, cols]
                m_far = logits[g] if m_far is None else jnp.maximum(m_far, logits[g])
            elif cls == "near":
                logits[g] = s[rows, cols] + bias_ref[rows, cols]
                m_near = logits[g] if m_near is None else jnp.maximum(m_near, logits[g])
        m = jnp.max(jnp.maximum(m_far + c_far, m_near), axis=-1, keepdims=True)
        m = jnp.broadcast_to(m, (CHUNK, 128))
        m_minus_c = m - c_far
        parts = []
        for g, cls in enumerate(classes):
            if cls == "far":
                parts.append(jnp.exp2(logits[g] - m_minus_c).astype(BF16))
            elif cls == "near":
                parts.append(jnp.exp2(logits[g] - m).astype(BF16))
            else:
                parts.append(jnp.zeros((CHUNK, 128), BF16))
        p_rows.append(jnp.concatenate(parts, axis=1))
    p = jnp.concatenate(p_rows, axis=0)
    o_ext = jnp.dot(p, v_ext, preferred_element_type=F32)
    o = o_ext[:, :ATT_HEAD_DIM] / o_ext[:, ATT_HEAD_DIM:]
    return (o * sg.astype(F32)).astype(BF16)


def _attention_kernel(q_ref, k_ref, v_ref, sg_ref, rows_ref, o_ref, bias_ref):
    left = ATT_KWIN - ATT_QBLOCK
    n_blocks = q_ref.shape[0] // ATT_QBLOCK
    first_full = left // ATT_QBLOCK

    @pl.when(pl.program_id(1) == 0)
    def _():
        qi = lax.broadcasted_iota(jnp.int32, (ATT_QBLOCK, ATT_KWIN), 0)
        kj = lax.broadcasted_iota(jnp.int32, (ATT_QBLOCK, ATT_KWIN), 1)
        q_chunk = (qi + left) // CHUNK
        k_chunk = kj // CHUNK
        band = (k_chunk <= q_chunk) & (k_chunk >= q_chunk - LEFT_CHUNKS)
        for hh in range(2):
            tiled = jnp.broadcast_to(rows_ref[hh], (ATT_QBLOCK, ATT_BIAS_ROW))
            toeplitz = pltpu.roll(tiled, 0, 1, stride=1, stride_axis=0)
            bias_ref[hh] = jnp.where(band, toeplitz[:, :ATT_KWIN], NEG_INF)

    ones = jnp.ones((ATT_KWIN, ATT_HEAD_DIM), BF16)

    def run(row_start, key_start, n_keys):
        rows = pl.ds(row_start, ATT_QBLOCK)
        keys = pl.ds(key_start, n_keys)
        for hh in range(2):
            cols = slice(hh * ATT_HEAD_DIM, (hh + 1) * ATT_HEAD_DIM)
            v_ext = jnp.concatenate([v_ref[keys, cols], ones[:n_keys]], axis=1)
            if n_keys == ATT_KWIN:
                o_ref[rows, cols] = _attend_full(
                    q_ref[rows, cols], k_ref[keys, cols], v_ext, bias_ref.at[hh], sg_ref[rows, cols])
            else:
                o_ref[rows, cols] = _attend(
                    q_ref[rows, cols], k_ref[keys, cols], v_ext,
                    bias_ref[hh, :, ATT_KWIN - n_keys:], sg_ref[rows, cols])

    for blk in range(first_full):
        run(blk * ATT_QBLOCK, 0, (blk + 1) * ATT_QBLOCK)

    def body(blk, carry):
        row_start = pl.multiple_of(blk * ATT_QBLOCK, ATT_QBLOCK)
        run(row_start, pl.multiple_of(row_start - left, ATT_QBLOCK), ATT_KWIN)
        return carry

    lax.fori_loop(first_full, n_blocks, body, 0, unroll=ATT_UNROLL)


def _bias_rows(rel_table):
    left = ATT_KWIN - ATT_QBLOCK
    t = rel_table.astype(F32) * LOG2E
    h = t.shape[0]
    far = jnp.broadcast_to(t[:, -1:], (h, left - MAX_REL))
    near = jnp.broadcast_to(t[:, :1], (h, ATT_KWIN - (left - MAX_REL) - t.shape[1]))
    wrap = jnp.broadcast_to(t[:, -1:], (h, ATT_BIAS_ROW - ATT_KWIN))
    return jnp.concatenate([far, t[:, ::-1], near, wrap], axis=1).reshape(h, 1, ATT_BIAS_ROW)


def _attention(q, k, v, sg, rel_table, batch, seq):
    n, width = q.shape
    assert seq % ATT_QBLOCK == 0
    pair = 2 * ATT_HEAD_DIM
    seq_pair = pl.BlockSpec((seq, pair), lambda h, b: (b, h))
    return pl.pallas_call(
        _attention_kernel,
        out_shape=jax.ShapeDtypeStruct((n, width), BF16),
        grid=(ATT_HEADS // 2, batch),
        in_specs=[seq_pair, seq_pair, seq_pair, seq_pair,
                  pl.BlockSpec((2, 1, ATT_BIAS_ROW), lambda h, b: (h, 0, 0))],
        out_specs=seq_pair,
        scratch_shapes=[pltpu.VMEM((2, ATT_QBLOCK, ATT_KWIN), F32)],
        compiler_params=_params(("arbitrary", "arbitrary")),
        name="attention",
    )(q, k, v, sg, _bias_rows(rel_table))


def kernel(x, c, positions, norm_g, ada_w, ada_b, ret_w_in, ret_gn_g, ret_w_out,
           att_w_in, att_q_g, att_k_g, att_rel_bias, att_w_out):
    batch, seq, d = x.shape
    n = batch * seq
    mod = _adaln(c, ada_w, ada_b)
    shift = mod[:, :, :d].reshape(-1, batch, 1, d)
    scale = mod[:, :, d:2 * d].reshape(-1, batch, 1, d)
    gate = mod[:, :, 2 * d:].reshape(-1, batch, 1, d)

    x2 = x.reshape(n, d)
    pos2 = positions.reshape(n, 1)
    freq = (1.0 / (ROPE_BASE ** (jnp.arange(0, RET_QK_DIM, 2, dtype=F32) / RET_QK_DIM))).reshape(1, -1)

    proj0 = functools.partial(_ret_proj, x2, pos2, freq, norm_g[0].reshape(1, d), shift[0], scale[0],
                              ret_w_in[0].astype(BF16), seq)
    small = jnp.max(jnp.abs(positions)) < SMALL_ANGLE_LIMIT
    q, k, v, sg = lax.cond(small, lambda: proj0(True), lambda: proj0(False))
    x2 = _retention(q, k, v, sg, ret_gn_g[0], ret_w_out[0].astype(BF16), x2, gate[0], batch, seq)

    q, k, v, sg = _att_proj(x2, norm_g[1].reshape(1, d), shift[1], scale[1],
                            att_w_in[0].astype(BF16), att_q_g[0].reshape(1, -1),
                            att_k_g[0].reshape(1, -1), seq)
    o = _attention(q, k, v, sg, att_rel_bias[0], batch, seq)
    x2 = _out_proj(o, att_w_out[0].astype(BF16), x2, gate[1], seq)
    return x2.reshape(batch, seq, d)
```

```python
import functools
import itertools
import math

import jax
import jax.numpy as jnp
from jax import lax
from jax.experimental import pallas as pl
from jax.experimental.pallas import tpu as pltpu

EPS = 1e-6
CHUNK = 64

RET_HEADS = 4
RET_QK_DIM = 256
RET_V_DIM = 512
ROPE_BASE = 10000.0

ATT_HEADS = 16
ATT_HEAD_DIM = 128
LEFT_CHUNKS = 8
MAX_REL = 2 * CHUNK
NEG_INF = -1e30

PROJ_ROWS = 512
OUT_PROJ_ROWS = 1024
RET_BLOCK = 256
RET_STEP_ROWS = 1024
ATT_QBLOCK = 256
ATT_KWIN = ATT_QBLOCK + LEFT_CHUNKS * CHUNK
ATT_BIAS_ROW = 1024
LOG2E = math.log2(math.e)

VMEM_LIMIT_BYTES = 56 * 1024 * 1024

BF16 = jnp.bfloat16
F32 = jnp.float32


def _resident(shape):
    return pl.BlockSpec(shape, lambda *_: (0,) * len(shape), pipeline_mode=pl.Buffered(1))


def _params(semantics):
    return pltpu.CompilerParams(dimension_semantics=semantics, vmem_limit_bytes=VMEM_LIMIT_BYTES)


def _adaln_kernel(c_ref, w_ref, b_ref, o_ref):
    c = c_ref[...]
    cond = c * (1.0 / (1.0 + jnp.exp(-c)))
    o_ref[0] = jnp.dot(cond, w_ref[0], preferred_element_type=F32) + b_ref[0]


def _adaln(c, ada_w, ada_b):
    depth, d, d3 = ada_w.shape
    b = c.shape[0]
    col = 1024
    return pl.pallas_call(
        _adaln_kernel,
        out_shape=jax.ShapeDtypeStruct((depth, b, d3), F32),
        grid=(depth, d3 // col),
        in_specs=[
            pl.BlockSpec((b, d), lambda i, j: (0, 0)),
            pl.BlockSpec((1, d, col), lambda i, j: (i, 0, j)),
            pl.BlockSpec((1, 1, col), lambda i, j: (i, 0, j)),
        ],
        out_specs=pl.BlockSpec((1, b, col), lambda i, j: (i, 0, j)),
        compiler_params=_params(("arbitrary", "arbitrary")),
        name="adaln",
    )(c, ada_w, ada_b.reshape(depth, 1, d3))


PIECES_AFTER_BUSY_DOT = 1
PIECES_AFTER_CAST_DOT = 5
PROLOGUE_PIECES = 12 * PIECES_AFTER_BUSY_DOT + 4 * PIECES_AFTER_CAST_DOT


def _modulated_norm(x, g_ref, shift_ref, scale_ref):
    ms = jnp.mean(x * x, axis=-1, keepdims=True)
    h = x * lax.rsqrt(ms + EPS) * g_ref[...]
    h = h * (1.0 + scale_ref[0]) + shift_ref[0]
    return h.astype(BF16)


def _piece_rows(piece, tile_rows):
    n = tile_rows // PROLOGUE_PIECES
    return slice(piece * n, (piece + 1) * n)


def _silu(t):
    half = 0.5 * t
    return half + half * jnp.tanh(half)


def _zero_from(*tiles):
    acc = None
    for t in tiles:
        bits = pltpu.bitcast(t, jnp.uint32)
        for r in range(0, bits.shape[0], 8):
            for c in range(0, bits.shape[1], 128):
                part = bits[r:r + 8, c:c + 128]
                acc = part if acc is None else acc | part
    return ((acc >> 16) >> 16).astype(F32)


def _dot_then_pieces(h_ref, w_ref, col, width, pieces, count, next_piece):
    acc = jnp.dot(h_ref[...], w_ref[:, col:col + width], preferred_element_type=F32)
    for piece in itertools.islice(pieces, count):
        top = jnp.concatenate([acc[:8, :128] + next_piece(piece), acc[:8, 128:]], axis=1)
        acc = jnp.concatenate([top, acc[8:]], axis=0)
    return acc


HALF_PI_PARTS = (1.578125, -0.00732421875, -4.470348358154297e-06, 1.5893254712295857e-08)
SMALL_ANGLE_LIMIT = 2 ** 17
SIN_TAYLOR = tuple((-1.0) ** i / math.factorial(2 * i + 1) for i in range(8))
COS_TAYLOR = tuple((-1.0) ** i / math.factorial(2 * i) for i in range(8))


def _sincos_small(x):
    k = jnp.floor(x * (1.0 / math.pi) + 0.5)
    r = x
    for part in HALF_PI_PARTS:
        r = r - k * (2.0 * part)
    z = r * r
    s = SIN_TAYLOR[-1]
    c = COS_TAYLOR[-1]
    for s_coef, c_coef in zip(SIN_TAYLOR[-2::-1], COS_TAYLOR[-2::-1]):
        s = s * z + s_coef
        c = c * z + c_coef
    sign = 1.0 - 2.0 * (k - 2.0 * jnp.floor(0.5 * k))
    return sign * c, sign * (s * r)


def _ret_prologue(piece, x_ref, pos_ref, freq_ref, g_ref, shift_ref, scale_ref, h_ref, cos_ref, sin_ref,
                  small_angles):
    rows = _piece_rows(piece, x_ref.shape[0])
    h = _modulated_norm(x_ref[rows, :], g_ref, shift_ref, scale_ref)
    h_ref[rows, :] = h
    ang = pos_ref[rows, :].astype(F32) * freq_ref[...]
    if small_angles:
        cos, sin = _sincos_small(ang)
    else:
        cos = jnp.cos(ang)
        sin = jnp.sin(ang)
    cos_ref[rows, :] = cos
    sin_ref[rows, :] = sin
    return _zero_from(h, cos, sin)


def _ret_project(h_ref, cos_ref, sin_ref, w_ref, q_ref, k_ref, v_ref, sg_ref, next_piece):
    cos = cos_ref[...]
    sin = sin_ref[...]
    half = RET_QK_DIM // 2
    qk_width = RET_HEADS * RET_QK_DIM
    v_width = RET_HEADS * RET_V_DIM
    pieces = iter(range(PROLOGUE_PIECES))
    proj = functools.partial(_dot_then_pieces, h_ref, w_ref, pieces=pieces, next_piece=next_piece)

    for out_ref, base in ((q_ref, 0), (k_ref, qk_width)):
        for hh in range(RET_HEADS):
            acc = proj(base + hh * RET_QK_DIM, RET_QK_DIM, count=PIECES_AFTER_BUSY_DOT)
            t1 = acc[:, :half]
            t2 = acc[:, half:]
            out_ref[:, hh * RET_QK_DIM:hh * RET_QK_DIM + half] = (t1 * cos - t2 * sin).astype(BF16)
            out_ref[:, hh * RET_QK_DIM + half:(hh + 1) * RET_QK_DIM] = (t1 * sin + t2 * cos).astype(BF16)
    for c in range(0, v_width, RET_V_DIM):
        acc = proj(2 * qk_width + v_width + c, RET_V_DIM, count=PIECES_AFTER_BUSY_DOT)
        sg_ref[:, c:c + RET_V_DIM] = _silu(acc).astype(BF16)
    for c in range(0, v_width, RET_V_DIM):
        acc = proj(2 * qk_width + c, RET_V_DIM, count=PIECES_AFTER_CAST_DOT)
        v_ref[:, c:c + RET_V_DIM] = acc.astype(BF16)
    assert next(pieces, None) is None


def _ret_proj_kernel(x0_ref, xn_ref, pos0_ref, posn_ref, freq_ref, g_ref,
                     shift0_ref, scale0_ref, shiftn_ref, scalen_ref, w_ref,
                     q_ref, k_ref, v_ref, sg_ref,
                     h_a, cos_a, sin_a, h_b, cos_b, sin_b, *, small_angles):
    step = pl.program_id(0)

    @pl.when(step == 0)
    def _():
        for piece in range(PROLOGUE_PIECES):
            _ret_prologue(piece, x0_ref, pos0_ref, freq_ref, g_ref, shift0_ref, scale0_ref, h_a, cos_a, sin_a,
                          small_angles)

    def run(cur, nxt):
        next_piece = functools.partial(_ret_prologue, x_ref=xn_ref, pos_ref=posn_ref, freq_ref=freq_ref,
                                       g_ref=g_ref, shift_ref=shiftn_ref, scale_ref=scalen_ref,
                                       h_ref=nxt[0], cos_ref=nxt[1], sin_ref=nxt[2], small_angles=small_angles)
        _ret_project(*cur, w_ref, q_ref, k_ref, v_ref, sg_ref, next_piece)

    set_a = (h_a, cos_a, sin_a)
    set_b = (h_b, cos_b, sin_b)

    @pl.when(step % 2 == 0)
    def _():
        run(set_a, set_b)

    @pl.when(step % 2 == 1)
    def _():
        run(set_b, set_a)


def _ret_proj(x2, pos2, freq, norm_g, shift, scale, w_bf16, seq, small_angles):
    n, d = x2.shape
    tm = PROJ_ROWS
    assert n % tm == 0 and seq % tm == 0
    per_seq = seq // tm
    last = n // tm - 1
    qk_width = RET_HEADS * RET_QK_DIM
    v_width = RET_HEADS * RET_V_DIM
    half = RET_QK_DIM // 2
    row = lambda i: (i, 0)
    nxt = lambda i: (jnp.minimum(i + 1, last), 0)
    nxt_bat = lambda i: (jnp.minimum(i + 1, last) // per_seq, 0, 0)
    return pl.pallas_call(
        functools.partial(_ret_proj_kernel, small_angles=small_angles),
        out_shape=(
            jax.ShapeDtypeStruct((n, qk_width), BF16),
            jax.ShapeDtypeStruct((n, qk_width), BF16),
            jax.ShapeDtypeStruct((n, v_width), BF16),
            jax.ShapeDtypeStruct((n, v_width), BF16),
        ),
        grid=(n // tm,),
        in_specs=[
            _resident((tm, d)),
            pl.BlockSpec((tm, d), nxt),
            _resident((tm, 1)),
            pl.BlockSpec((tm, 1), nxt),
            _resident((1, half)),
            _resident((1, d)),
            _resident((1, 1, d)),
            _resident((1, 1, d)),
            pl.BlockSpec((1, 1, d), nxt_bat),
            pl.BlockSpec((1, 1, d), nxt_bat),
            _resident(w_bf16.shape),
        ],
        out_specs=(
            pl.BlockSpec((tm, qk_width), row),
            pl.BlockSpec((tm, qk_width), row),
            pl.BlockSpec((tm, v_width), row),
            pl.BlockSpec((tm, v_width), row),
        ),
        scratch_shapes=[pltpu.VMEM((tm, d), BF16), pltpu.VMEM((tm, half), F32), pltpu.VMEM((tm, half), F32),
                        pltpu.VMEM((tm, d), BF16), pltpu.VMEM((tm, half), F32), pltpu.VMEM((tm, half), F32)],
        compiler_params=_params(("arbitrary",)),
        name="ret_proj",
    )(x2, x2, pos2, pos2, freq, norm_g, shift, scale, shift, scale, w_bf16)


def _retention_kernel(q_ref, k_ref, v_ref, sg_ref, dmask_ref, qd_ref, kd_ref, bd_ref, gn_ref,
                      w_ref, x_ref, gate_ref, y_ref, state_ref):
    @pl.when(pl.program_id(1) == 0)
    def _():
        state_ref[...] = jnp.zeros_like(state_ref)

    for blk in range(RET_STEP_ROWS // RET_BLOCK):
        rows = slice(blk * RET_BLOCK, (blk + 1) * RET_BLOCK)
        gated = []
        for head in range(RET_HEADS):
            qk_cols = slice(head * RET_QK_DIM, (head + 1) * RET_QK_DIM)
            v_cols = slice(head * RET_V_DIM, (head + 1) * RET_V_DIM)
            q = q_ref[rows, qk_cols]
            k = k_ref[rows, qk_cols]
            v = v_ref[rows, v_cols]
            state = state_ref[head]
            scores = lax.dot_general(q, k, (((1,), (1,)), ((), ())), preferred_element_type=F32)
            o = jnp.dot((scores * dmask_ref[head]).astype(BF16), v, preferred_element_type=F32)
            qs = q * qd_ref[head]
            o = o + jnp.dot(qs, state.astype(BF16), preferred_element_type=F32)
            ks = k * kd_ref[head]
            state_ref[head] = bd_ref[head] * state + lax.dot_general(
                ks, v, (((0,), (0,)), ((), ())), preferred_element_type=F32)
            ms = jnp.mean(o * o, axis=-1, keepdims=True)
            o = o * lax.rsqrt(ms + EPS) * gn_ref[head]
            gated.append(o.astype(BF16) * sg_ref[rows, v_cols])
        out = jnp.dot(jnp.concatenate(gated, axis=1), w_ref[...], preferred_element_type=F32)
        y_ref[rows, :] = x_ref[rows, :] + gate_ref[0] * out


def _retention_tables():
    log_gamma = jnp.log(1.0 - 2.0 ** (-5.0 - jnp.arange(RET_HEADS, dtype=F32)))
    t = jnp.arange(RET_BLOCK, dtype=F32)
    dist = t[:, None] - t[None, :]
    chunk = jnp.arange(RET_BLOCK) // CHUNK
    visible = chunk[None, :] <= chunk[:, None]
    k_scale = RET_QK_DIM ** -0.5
    dmask = jnp.where(visible[None], jnp.exp(log_gamma[:, None, None] * jnp.abs(dist)[None]), 0.0) * k_scale
    qd = jnp.exp(log_gamma[:, None] * (t + 1.0))
    kd = jnp.exp(log_gamma[:, None] * (RET_BLOCK - 1.0 - t)) * k_scale
    qd = jnp.broadcast_to(qd[:, :, None], (RET_HEADS, RET_BLOCK, RET_QK_DIM))
    kd = jnp.broadcast_to(kd[:, :, None], (RET_HEADS, RET_BLOCK, RET_QK_DIM))
    bd = jnp.broadcast_to(jnp.exp(log_gamma * RET_BLOCK)[:, None, None], (RET_HEADS, 1, RET_V_DIM))
    return dmask.astype(F32), qd.astype(BF16), kd.astype(BF16), bd.astype(F32)


def _retention(q, k, v, sg, gn_g, w_out_bf16, x2, gate, batch, seq):
    n, d = x2.shape
    rows = RET_STEP_ROWS
    assert seq % rows == 0
    per_seq = seq // rows
    dmask, qd, kd, bd = _retention_tables()
    qk_width = RET_HEADS * RET_QK_DIM
    v_width = RET_HEADS * RET_V_DIM
    row = lambda b, j: (b * per_seq + j, 0)
    return pl.pallas_call(
        _retention_kernel,
        out_shape=jax.ShapeDtypeStruct((n, d), F32),
        grid=(batch, per_seq),
        in_specs=[
            pl.BlockSpec((rows, qk_width), row),
            pl.BlockSpec((rows, qk_width), row),
            pl.BlockSpec((rows, v_width), row),
            pl.BlockSpec((rows, v_width), row),
            _resident(dmask.shape),
            _resident(qd.shape),
            _resident(kd.shape),
            _resident(bd.shape),
            _resident((RET_HEADS, 1, RET_V_DIM)),
            _resident(w_out_bf16.shape),
            pl.BlockSpec((rows, d), row),
            pl.BlockSpec((1, 1, d), lambda b, j: (b, 0, 0)),
        ],
        out_specs=pl.BlockSpec((rows, d), row),
        scratch_shapes=[pltpu.VMEM((RET_HEADS, RET_QK_DIM, RET_V_DIM), F32)],
        compiler_params=_params(("arbitrary", "arbitrary")),
        name="retention",
    )(q, k, v, sg, dmask, qd, kd, bd, gn_g.reshape(RET_HEADS, 1, RET_V_DIM), w_out_bf16, x2, gate)


def _out_proj_kernel(o_ref, w_ref, x_ref, gate_ref, y_ref):
    out = jnp.dot(o_ref[...], w_ref[...], preferred_element_type=F32)
    y_ref[...] = x_ref[...] + gate_ref[0] * out


def _out_proj(o, w_bf16, x2, gate, seq):
    n, d = x2.shape
    width = o.shape[1]
    tm = OUT_PROJ_ROWS
    assert n % tm == 0 and seq % tm == 0
    per_seq = seq // tm
    row = lambda i: (i, 0)
    return pl.pallas_call(
        _out_proj_kernel,
        out_shape=jax.ShapeDtypeStruct((n, d), F32),
        grid=(n // tm,),
        in_specs=[
            pl.BlockSpec((tm, width), row),
            _resident(w_bf16.shape),
            pl.BlockSpec((tm, d), row),
            pl.BlockSpec((1, 1, d), lambda i: (i // per_seq, 0, 0)),
        ],
        out_specs=pl.BlockSpec((tm, d), row),
        compiler_params=_params(("arbitrary",)),
        name="out_proj",
    )(o, w_bf16, x2, gate)


def _att_prologue(piece, x_ref, g_ref, shift_ref, scale_ref, h_ref):
    rows = _piece_rows(piece, x_ref.shape[0])
    h = _modulated_norm(x_ref[rows, :], g_ref, shift_ref, scale_ref)
    h_ref[rows, :] = h
    return _zero_from(h)


def _att_project(h_ref, w_ref, qg_ref, kg_ref, q_ref, k_ref, v_ref, sg_ref, next_piece):
    width = ATT_HEADS * ATT_HEAD_DIM
    chunk = 512
    pieces = iter(range(PROLOGUE_PIECES))
    proj = functools.partial(_dot_then_pieces, h_ref, w_ref, pieces=pieces, next_piece=next_piece)

    for out_ref, gain_ref, base, post in ((q_ref, qg_ref, 0, ATT_HEAD_DIM ** -0.5 * LOG2E), (k_ref, kg_ref, width, 1.0)):
        gain = gain_ref[...] * post
        for c in range(0, width, chunk):
            acc = proj(base + c, chunk, count=PIECES_AFTER_BUSY_DOT)
            for hh in range(chunk // ATT_HEAD_DIM):
                a = acc[:, hh * ATT_HEAD_DIM:(hh + 1) * ATT_HEAD_DIM]
                ms = jnp.mean(a * a, axis=-1, keepdims=True)
                out_ref[:, c + hh * ATT_HEAD_DIM:c + (hh + 1) * ATT_HEAD_DIM] = (
                    a * lax.rsqrt(ms + EPS) * gain).astype(BF16)
    for c in range(0, width, chunk):
        sg_ref[:, c:c + chunk] = _silu(proj(3 * width + c, chunk, count=PIECES_AFTER_BUSY_DOT)).astype(BF16)
    for c in range(0, width, chunk):
        v_ref[:, c:c + chunk] = proj(2 * width + c, chunk, count=PIECES_AFTER_CAST_DOT).astype(BF16)
    assert next(pieces, None) is None


def _att_proj_kernel(x0_ref, xn_ref, g_ref, shift0_ref, scale0_ref, shiftn_ref, scalen_ref,
                     w_ref, qg_ref, kg_ref, q_ref, k_ref, v_ref, sg_ref, h_a, h_b):
    step = pl.program_id(0)

    @pl.when(step == 0)
    def _():
        for piece in range(PROLOGUE_PIECES):
            _att_prologue(piece, x0_ref, g_ref, shift0_ref, scale0_ref, h_a)

    def run(cur, nxt):
        next_piece = functools.partial(_att_prologue, x_ref=xn_ref, g_ref=g_ref, shift_ref=shiftn_ref,
                                       scale_ref=scalen_ref, h_ref=nxt)
        _att_project(cur, w_ref, qg_ref, kg_ref, q_ref, k_ref, v_ref, sg_ref, next_piece)

    @pl.when(step % 2 == 0)
    def _():
        run(h_a, h_b)

    @pl.when(step % 2 == 1)
    def _():
        run(h_b, h_a)


def _att_proj(x2, norm_g, shift, scale, w_bf16, q_g, k_g, seq):
    n, d = x2.shape
    tm = PROJ_ROWS
    per_seq = seq // tm
    last = n // tm - 1
    width = ATT_HEADS * ATT_HEAD_DIM
    row = lambda i: (i, 0)
    nxt = lambda i: (jnp.minimum(i + 1, last), 0)
    nxt_bat = lambda i: (jnp.minimum(i + 1, last) // per_seq, 0, 0)
    out = jax.ShapeDtypeStruct((n, width), BF16)
    return pl.pallas_call(
        _att_proj_kernel,
        out_shape=(out, out, out, out),
        grid=(n // tm,),
        in_specs=[
            _resident((tm, d)),
            pl.BlockSpec((tm, d), nxt),
            _resident((1, d)),
            _resident((1, 1, d)),
            _resident((1, 1, d)),
            pl.BlockSpec((1, 1, d), nxt_bat),
            pl.BlockSpec((1, 1, d), nxt_bat),
            _resident(w_bf16.shape),
            _resident((1, ATT_HEAD_DIM)),
            _resident((1, ATT_HEAD_DIM)),
        ],
        out_specs=tuple(pl.BlockSpec((tm, width), row) for _ in range(4)),
        scratch_shapes=[pltpu.VMEM((tm, d), BF16), pltpu.VMEM((tm, d), BF16)],
        compiler_params=_params(("arbitrary",)),
        name="att_proj",
    )(x2, x2, norm_g, shift, scale, shift, scale, w_bf16, q_g, k_g)


def _attend(q, k, v_ext, bias, sg):
    s = lax.dot_general(q, k, (((1,), (1,)), ((), ())), preferred_element_type=F32) + bias
    m = jnp.max(s, axis=-1, keepdims=True)
    p = jnp.exp2(s - m).astype(BF16)
    o_ext = jnp.dot(p, v_ext, preferred_element_type=F32)
    o = o_ext[:, :ATT_HEAD_DIM] / o_ext[:, ATT_HEAD_DIM:]
    return (o * sg.astype(F32)).astype(BF16)


def _window_classes():
    chunks = ATT_QBLOCK // CHUNK
    far_chunks = LEFT_CHUNKS - MAX_REL // CHUNK
    table = []
    for ci in range(chunks):
        row = []
        for g in range(ATT_KWIN // 128):
            key_chunks = (2 * g, 2 * g + 1)
            valid = [ci <= ch <= ci + LEFT_CHUNKS for ch in key_chunks]
            far = [ci <= ch <= ci + far_chunks - 1 for ch in key_chunks]
            row.append("skip" if not any(valid) else "far" if all(far) else "near")
        table.append(row)
    return table


def _attend_full(q, k, v_ext, bias_ref, sg):
    s = lax.dot_general(q, k, (((1,), (1,)), ((), ())), preferred_element_type=F32)
    c_far = bias_ref[0:CHUNK, 0:128]
    p_rows = []
    for ci, classes in enumerate(_window_classes()):
        rows = slice(ci * CHUNK, (ci + 1) * CHUNK)
        logits = {}
        m_far = None
        m_near = None
        for g, cls in enumerate(classes):
            cols = slice(g * 128, (g + 1) * 128)
            if cls == "far":
                logits[g] = s[rows, cols]
                m_far = logits[g] if m_far is None else jnp.maximum(m_far, logits[g])
            elif cls == "near":
                logits[g] = s[rows, cols] + bias_ref[rows, cols]
                m_near = logits[g] if m_near is None else jnp.maximum(m_near, logits[g])
        m = jnp.max(jnp.maximum(m_far + c_far, m_near), axis=-1, keepdims=True)
        m = jnp.broadcast_to(m, (CHUNK, 128))
        m_minus_c = m - c_far
        parts = []
        for g, cls in enumerate(classes):
            if cls == "far":
                parts.append(jnp.exp2(logits[g] - m_minus_c).astype(BF16))
            elif cls == "near":
                parts.append(jnp.exp2(logits[g] - m).astype(BF16))
            else:
                parts.append(jnp.zeros((CHUNK, 128), BF16))
        p_rows.append(jnp.concatenate(parts, axis=1))
    p = jnp.concatenate(p_rows, axis=0)
    o_ext = jnp.dot(p, v_ext, preferred_element_type=F32)
    o = o_ext[:, :ATT_HEAD_DIM] / o_ext[:, ATT_HEAD_DIM:]
    return (o * sg.astype(F32)).astype(BF16)


def _attention_kernel(q_ref, k_ref, v_ref, sg_ref, rows_ref, o_ref, bias_ref):
    left = ATT_KWIN - ATT_QBLOCK
    n_blocks = q_ref.shape[0] // ATT_QBLOCK

    @pl.when(pl.program_id(1) == 0)
    def _():
        qi = lax.broadcasted_iota(jnp.int32, (ATT_QBLOCK, ATT_KWIN), 0)
        kj = lax.broadcasted_iota(jnp.int32, (ATT_QBLOCK, ATT_KWIN), 1)
        q_chunk = (qi + left) // CHUNK
        k_chunk = kj // CHUNK
        band = (k_chunk <= q_chunk) & (k_chunk >= q_chunk - LEFT_CHUNKS)
        for hh in range(2):
            tiled = jnp.broadcast_to(rows_ref[hh], (ATT_QBLOCK, ATT_BIAS_ROW))
            toeplitz = pltpu.roll(tiled, 0, 1, stride=1, stride_axis=0)
            bias_ref[hh] = jnp.where(band, toeplitz[:, :ATT_KWIN], NEG_INF)

    ones = jnp.ones((ATT_KWIN, ATT_HEAD_DIM), BF16)

    def run(row_start, key_start, n_keys):
        rows = pl.ds(row_start, ATT_QBLOCK)
        keys = pl.ds(key_start, n_keys)
        for hh in range(2):
            cols = slice(hh * ATT_HEAD_DIM, (hh + 1) * ATT_HEAD_DIM)
            v_ext = jnp.concatenate([v_ref[keys, cols], ones[:n_keys]], axis=1)
            if n_keys == ATT_KWIN:
                o_ref[rows, cols] = _attend_full(
                    q_ref[rows, cols], k_ref[keys, cols], v_ext, bias_ref.at[hh], sg_ref[rows, cols])
            else:
                o_ref[rows, cols] = _attend(
                    q_ref[rows, cols], k_ref[keys, cols], v_ext,
                    bias_ref[hh, :, ATT_KWIN - n_keys:], sg_ref[rows, cols])

    for blk in range(n_blocks):
        key_start = max(blk * ATT_QBLOCK - left, 0)
        run(blk * ATT_QBLOCK, key_start, (blk + 1) * ATT_QBLOCK - key_start)


def _bias_rows(rel_table):
    left = ATT_KWIN - ATT_QBLOCK
    t = rel_table.astype(F32) * LOG2E
    h = t.shape[0]
    far = jnp.broadcast_to(t[:, -1:], (h, left - MAX_REL))
    near = jnp.broadcast_to(t[:, :1], (h, ATT_KWIN - (left - MAX_REL) - t.shape[1]))
    wrap = jnp.broadcast_to(t[:, -1:], (h, ATT_BIAS_ROW - ATT_KWIN))
    return jnp.concatenate([far, t[:, ::-1], near, wrap], axis=1).reshape(h, 1, ATT_BIAS_ROW)


def _attention(q, k, v, sg, rel_table, batch, seq):
    n, width = q.shape
    assert seq % ATT_QBLOCK == 0
    pair = 2 * ATT_HEAD_DIM
    seq_pair = pl.BlockSpec((seq, pair), lambda h, b: (b, h))
    return pl.pallas_call(
        _attention_kernel,
        out_shape=jax.ShapeDtypeStruct((n, width), BF16),
        grid=(ATT_HEADS // 2, batch),
        in_specs=[seq_pair, seq_pair, seq_pair, seq_pair,
                  pl.BlockSpec((2, 1, ATT_BIAS_ROW), lambda h, b: (h, 0, 0))],
        out_specs=seq_pair,
        scratch_shapes=[pltpu.VMEM((2, ATT_QBLOCK, ATT_KWIN), F32)],
        compiler_params=_params(("arbitrary", "arbitrary")),
        name="attention",
    )(q, k, v, sg, _bias_rows(rel_table))


def kernel(x, c, positions, norm_g, ada_w, ada_b, ret_w_in, ret_gn_g, ret_w_out,
           att_w_in, att_q_g, att_k_g, att_rel_bias, att_w_out):
    batch, seq, d = x.shape
    n = batch * seq
    mod = _adaln(c, ada_w, ada_b)
    shift = mod[:, :, :d].reshape(-1, batch, 1, d)
    scale = mod[:, :, d:2 * d].reshape(-1, batch, 1, d)
    gate = mod[:, :, 2 * d:].reshape(-1, batch, 1, d)

    x2 = x.reshape(n, d)
    pos2 = positions.reshape(n, 1)
    freq = (1.0 / (ROPE_BASE ** (jnp.arange(0, RET_QK_DIM, 2, dtype=F32) / RET_QK_DIM))).reshape(1, -1)

    proj0 = functools.partial(_ret_proj, x2, pos2, freq, norm_g[0].reshape(1, d), shift[0], scale[0],
                              ret_w_in[0].astype(BF16), seq)
    small = jnp.all((positions > -SMALL_ANGLE_LIMIT) & (positions < SMALL_ANGLE_LIMIT))
    q, k, v, sg = lax.cond(small, lambda: proj0(True), lambda: proj0(False))
    x2 = _retention(q, k, v, sg, ret_gn_g[0], ret_w_out[0].astype(BF16), x2, gate[0], batch, seq)

    q, k, v, sg = _att_proj(x2, norm_g[1].reshape(1, d), shift[1], scale[1],
                            att_w_in[0].astype(BF16), att_q_g[0].reshape(1, -1),
                            att_k_g[0].reshape(1, -1), seq)
    o = _attention(q, k, v, sg, att_rel_bias[0], batch, seq)
    x2 = _out_proj(o, att_w_out[0].astype(BF16), x2, gate[1], seq)
    return x2.reshape(batch, seq, d)
```

```python
import functools
import math

import jax
import jax.numpy as jnp
from jax import lax
from jax.experimental import pallas as pl
from jax.experimental.pallas import tpu as pltpu

EPS = 1e-6
CHUNK = 64

RET_HEADS = 4
RET_QK_DIM = 256
RET_V_DIM = 512
ROPE_BASE = 10000.0

ATT_HEADS = 16
ATT_HEAD_DIM = 128
LEFT_CHUNKS = 8
MAX_REL = 2 * CHUNK
NEG_INF = -1e30

PROJ_ROWS = 512
OUT_PROJ_ROWS = 1024
RET_BLOCK = 256
RET_STEP_ROWS = 1024
ATT_QBLOCK = 256
ATT_KWIN = ATT_QBLOCK + LEFT_CHUNKS * CHUNK
ATT_BIAS_ROW = 1024
LOG2E = math.log2(math.e)

VMEM_LIMIT_BYTES = 56 * 1024 * 1024

BF16 = jnp.bfloat16
F32 = jnp.float32


def _resident(shape):
    return pl.BlockSpec(shape, lambda *_: (0,) * len(shape), pipeline_mode=pl.Buffered(1))


def _params(semantics):
    return pltpu.CompilerParams(dimension_semantics=semantics, vmem_limit_bytes=VMEM_LIMIT_BYTES)


def _adaln_kernel(c_ref, w_ref, b_ref, o_ref):
    c = c_ref[...]
    cond = c * (1.0 / (1.0 + jnp.exp(-c)))
    o_ref[0] = jnp.dot(cond, w_ref[0], preferred_element_type=F32) + b_ref[0]


def _adaln(c, ada_w, ada_b):
    depth, d, d3 = ada_w.shape
    b = c.shape[0]
    col = 1024
    return pl.pallas_call(
        _adaln_kernel,
        out_shape=jax.ShapeDtypeStruct((depth, b, d3), F32),
        grid=(depth, d3 // col),
        in_specs=[
            pl.BlockSpec((b, d), lambda i, j: (0, 0)),
            pl.BlockSpec((1, d, col), lambda i, j: (i, 0, j)),
            pl.BlockSpec((1, 1, col), lambda i, j: (i, 0, j)),
        ],
        out_specs=pl.BlockSpec((1, b, col), lambda i, j: (i, 0, j)),
        compiler_params=_params(("arbitrary", "arbitrary")),
        name="adaln",
    )(c, ada_w, ada_b.reshape(depth, 1, d3))


def _modulated_norm(x, g_ref, shift_ref, scale_ref):
    ms = jnp.mean(x * x, axis=-1, keepdims=True)
    h = x * lax.rsqrt(ms + EPS) * g_ref[...]
    h = h * (1.0 + scale_ref[0]) + shift_ref[0]
    return h.astype(BF16)


def _silu(t):
    half = 0.5 * t
    return half + half * jnp.tanh(half)


HALF_PI_PARTS = (1.578125, -0.00732421875, -4.470348358154297e-06, 1.5893254712295857e-08)
SMALL_ANGLE_LIMIT = 2 ** 17
SIN_TAYLOR = tuple((-1.0) ** i / math.factorial(2 * i + 1) for i in range(8))
COS_TAYLOR = tuple((-1.0) ** i / math.factorial(2 * i) for i in range(8))


def _sincos_small(x):
    k = jnp.floor(x * (1.0 / math.pi) + 0.5)
    r = x
    for part in HALF_PI_PARTS:
        r = r - k * (2.0 * part)
    z = r * r
    s = SIN_TAYLOR[-1]
    c = COS_TAYLOR[-1]
    for s_coef, c_coef in zip(SIN_TAYLOR[-2::-1], COS_TAYLOR[-2::-1]):
        s = s * z + s_coef
        c = c * z + c_coef
    sign = 1.0 - 2.0 * (k - 2.0 * jnp.floor(0.5 * k))
    return sign * c, sign * (s * r)


def _ret_proj_kernel(x_ref, pos_ref, freq_ref, g_ref, shift_ref, scale_ref, w_ref,
                     q_ref, k_ref, v_ref, sg_ref, h_ref, cos_ref, sin_ref, *, small_angles):
    h_ref[...] = _modulated_norm(x_ref[...], g_ref, shift_ref, scale_ref)
    ang = pos_ref[...].astype(F32) * freq_ref[...]
    if small_angles:
        cos_ref[...], sin_ref[...] = _sincos_small(ang)
    else:
        cos_ref[...] = jnp.cos(ang)
        sin_ref[...] = jnp.sin(ang)
    cos = cos_ref[...]
    sin = sin_ref[...]
    half = RET_QK_DIM // 2
    qk_width = RET_HEADS * RET_QK_DIM
    v_width = RET_HEADS * RET_V_DIM

    def proj(col, width):
        return jnp.dot(h_ref[...], w_ref[:, col:col + width], preferred_element_type=F32)

    for out_ref, base in ((q_ref, 0), (k_ref, qk_width)):
        for hh in range(RET_HEADS):
            acc = proj(base + hh * RET_QK_DIM, RET_QK_DIM)
            t1 = acc[:, :half]
            t2 = acc[:, half:]
            out_ref[:, hh * RET_QK_DIM:hh * RET_QK_DIM + half] = (t1 * cos - t2 * sin).astype(BF16)
            out_ref[:, hh * RET_QK_DIM + half:(hh + 1) * RET_QK_DIM] = (t1 * sin + t2 * cos).astype(BF16)
    for c in range(0, v_width, RET_V_DIM):
        sg_ref[:, c:c + RET_V_DIM] = _silu(proj(2 * qk_width + v_width + c, RET_V_DIM)).astype(BF16)
    for c in range(0, v_width, RET_V_DIM):
        v_ref[:, c:c + RET_V_DIM] = proj(2 * qk_width + c, RET_V_DIM).astype(BF16)


def _ret_proj(x2, pos2, freq, norm_g, shift, scale, w_bf16, seq, small_angles):
    n, d = x2.shape
    tm = PROJ_ROWS
    assert n % tm == 0 and seq % tm == 0
    per_seq = seq // tm
    qk_width = RET_HEADS * RET_QK_DIM
    v_width = RET_HEADS * RET_V_DIM
    half = RET_QK_DIM // 2
    row = lambda i: (i, 0)
    bat = lambda i: (i // per_seq, 0, 0)
    return pl.pallas_call(
        functools.partial(_ret_proj_kernel, small_angles=small_angles),
        out_shape=(
            jax.ShapeDtypeStruct((n, qk_width), BF16),
            jax.ShapeDtypeStruct((n, qk_width), BF16),
            jax.ShapeDtypeStruct((n, v_width), BF16),
            jax.ShapeDtypeStruct((n, v_width), BF16),
        ),
        grid=(n // tm,),
        in_specs=[
            pl.BlockSpec((tm, d), row),
            pl.BlockSpec((tm, 1), row),
            _resident((1, half)),
            _resident((1, d)),
            pl.BlockSpec((1, 1, d), bat),
            pl.BlockSpec((1, 1, d), bat),
            _resident(w_bf16.shape),
        ],
        out_specs=(
            pl.BlockSpec((tm, qk_width), row),
            pl.BlockSpec((tm, qk_width), row),
            pl.BlockSpec((tm, v_width), row),
            pl.BlockSpec((tm, v_width), row),
        ),
        scratch_shapes=[pltpu.VMEM((tm, d), BF16), pltpu.VMEM((tm, half), F32), pltpu.VMEM((tm, half), F32)],
        compiler_params=_params(("arbitrary",)),
        name="ret_proj",
    )(x2, pos2, freq, norm_g, shift, scale, w_bf16)


def _retention_kernel(q_ref, k_ref, v_ref, sg_ref, dmask_ref, qd_ref, kd_ref, bd_ref, gn_ref,
                      w_ref, x_ref, gate_ref, y_ref, state_ref):
    @pl.when(pl.program_id(1) == 0)
    def _():
        state_ref[...] = jnp.zeros_like(state_ref)

    for blk in range(RET_STEP_ROWS // RET_BLOCK):
        rows = slice(blk * RET_BLOCK, (blk + 1) * RET_BLOCK)
        gated = []
        for head in range(RET_HEADS):
            qk_cols = slice(head * RET_QK_DIM, (head + 1) * RET_QK_DIM)
            v_cols = slice(head * RET_V_DIM, (head + 1) * RET_V_DIM)
            q = q_ref[rows, qk_cols]
            k = k_ref[rows, qk_cols]
            v = v_ref[rows, v_cols]
            state = state_ref[head]
            scores = lax.dot_general(q, k, (((1,), (1,)), ((), ())), preferred_element_type=F32)
            o = jnp.dot((scores * dmask_ref[head]).astype(BF16), v, preferred_element_type=F32)
            qs = q * qd_ref[head]
            o = o + jnp.dot(qs, state.astype(BF16), preferred_element_type=F32)
            ks = k * kd_ref[head]
            state_ref[head] = bd_ref[head] * state + lax.dot_general(
                ks, v, (((0,), (0,)), ((), ())), preferred_element_type=F32)
            ms = jnp.mean(o * o, axis=-1, keepdims=True)
            o = o * lax.rsqrt(ms + EPS) * gn_ref[head]
            gated.append(o.astype(BF16) * sg_ref[rows, v_cols])
        out = jnp.dot(jnp.concatenate(gated, axis=1), w_ref[...], preferred_element_type=F32)
        y_ref[rows, :] = x_ref[rows, :] + gate_ref[0] * out


def _retention_tables():
    log_gamma = jnp.log(1.0 - 2.0 ** (-5.0 - jnp.arange(RET_HEADS, dtype=F32)))
    t = jnp.arange(RET_BLOCK, dtype=F32)
    dist = t[:, None] - t[None, :]
    chunk = jnp.arange(RET_BLOCK) // CHUNK
    visible = chunk[None, :] <= chunk[:, None]
    k_scale = RET_QK_DIM ** -0.5
    dmask = jnp.where(visible[None], jnp.exp(log_gamma[:, None, None] * jnp.abs(dist)[None]), 0.0) * k_scale
    qd = jnp.exp(log_gamma[:, None] * (t + 1.0))
    kd = jnp.exp(log_gamma[:, None] * (RET_BLOCK - 1.0 - t)) * k_scale
    qd = jnp.broadcast_to(qd[:, :, None], (RET_HEADS, RET_BLOCK, RET_QK_DIM))
    kd = jnp.broadcast_to(kd[:, :, None], (RET_HEADS, RET_BLOCK, RET_QK_DIM))
    bd = jnp.broadcast_to(jnp.exp(log_gamma * RET_BLOCK)[:, None, None], (RET_HEADS, 1, RET_V_DIM))
    return dmask.astype(F32), qd.astype(BF16), kd.astype(BF16), bd.astype(F32)


def _retention(q, k, v, sg, gn_g, w_out_bf16, x2, gate, batch, seq):
    n, d = x2.shape
    rows = RET_STEP_ROWS
    assert seq % rows == 0
    per_seq = seq // rows
    dmask, qd, kd, bd = _retention_tables()
    qk_width = RET_HEADS * RET_QK_DIM
    v_width = RET_HEADS * RET_V_DIM
    row = lambda b, j: (b * per_seq + j, 0)
    return pl.pallas_call(
        _retention_kernel,
        out_shape=jax.ShapeDtypeStruct((n, d), F32),
        grid=(batch, per_seq),
        in_specs=[
            pl.BlockSpec((rows, qk_width), row),
            pl.BlockSpec((rows, qk_width), row),
            pl.BlockSpec((rows, v_width), row),
            pl.BlockSpec((rows, v_width), row),
            _resident(dmask.shape),
            _resident(qd.shape),
            _resident(kd.shape),
            _resident(bd.shape),
            _resident((RET_HEADS, 1, RET_V_DIM)),
            _resident(w_out_bf16.shape),
            pl.BlockSpec((rows, d), row),
            pl.BlockSpec((1, 1, d), lambda b, j: (b, 0, 0)),
        ],
        out_specs=pl.BlockSpec((rows, d), row),
        scratch_shapes=[pltpu.VMEM((RET_HEADS, RET_QK_DIM, RET_V_DIM), F32)],
        compiler_params=_params(("arbitrary", "arbitrary")),
        name="retention",
    )(q, k, v, sg, dmask, qd, kd, bd, gn_g.reshape(RET_HEADS, 1, RET_V_DIM), w_out_bf16, x2, gate)


def _out_proj_kernel(o_ref, w_ref, x_ref, gate_ref, y_ref):
    out = jnp.dot(o_ref[...], w_ref[...], preferred_element_type=F32)
    y_ref[...] = x_ref[...] + gate_ref[0] * out


def _out_proj(o, w_bf16, x2, gate, seq):
    n, d = x2.shape
    width = o.shape[1]
    tm = OUT_PROJ_ROWS
    assert n % tm == 0 and seq % tm == 0
    per_seq = seq // tm
    row = lambda i: (i, 0)
    return pl.pallas_call(
        _out_proj_kernel,
        out_shape=jax.ShapeDtypeStruct((n, d), F32),
        grid=(n // tm,),
        in_specs=[
            pl.BlockSpec((tm, width), row),
            _resident(w_bf16.shape),
            pl.BlockSpec((tm, d), row),
            pl.BlockSpec((1, 1, d), lambda i: (i // per_seq, 0, 0)),
        ],
        out_specs=pl.BlockSpec((tm, d), row),
        compiler_params=_params(("arbitrary",)),
        name="out_proj",
    )(o, w_bf16, x2, gate)


def _att_proj_kernel(x_ref, g_ref, shift_ref, scale_ref, w_ref, qg_ref, kg_ref,
                     q_ref, k_ref, v_ref, sg_ref, h_ref):
    h_ref[...] = _modulated_norm(x_ref[...], g_ref, shift_ref, scale_ref)
    width = ATT_HEADS * ATT_HEAD_DIM
    chunk = 512

    def proj(col, w):
        return jnp.dot(h_ref[...], w_ref[:, col:col + w], preferred_element_type=F32)

    for out_ref, gain_ref, base, post in ((q_ref, qg_ref, 0, ATT_HEAD_DIM ** -0.5 * LOG2E), (k_ref, kg_ref, width, 1.0)):
        gain = gain_ref[...] * post
        for c in range(0, width, chunk):
            acc = proj(base + c, chunk)
            for hh in range(chunk // ATT_HEAD_DIM):
                a = acc[:, hh * ATT_HEAD_DIM:(hh + 1) * ATT_HEAD_DIM]
                ms = jnp.mean(a * a, axis=-1, keepdims=True)
                out_ref[:, c + hh * ATT_HEAD_DIM:c + (hh + 1) * ATT_HEAD_DIM] = (
                    a * lax.rsqrt(ms + EPS) * gain).astype(BF16)
    for c in range(0, width, chunk):
        sg_ref[:, c:c + chunk] = _silu(proj(3 * width + c, chunk)).astype(BF16)
    for c in range(0, width, chunk):
        v_ref[:, c:c + chunk] = proj(2 * width + c, chunk).astype(BF16)


def _att_proj(x2, norm_g, shift, scale, w_bf16, q_g, k_g, seq):
    n, d = x2.shape
    tm = PROJ_ROWS
    assert n % tm == 0 and seq % tm == 0
    per_seq = seq // tm
    width = ATT_HEADS * ATT_HEAD_DIM
    row = lambda i: (i, 0)
    bat = lambda i: (i // per_seq, 0, 0)
    out = jax.ShapeDtypeStruct((n, width), BF16)
    return pl.pallas_call(
        _att_proj_kernel,
        out_shape=(out, out, out, out),
        grid=(n // tm,),
        in_specs=[
            pl.BlockSpec((tm, d), row),
            _resident((1, d)),
            pl.BlockSpec((1, 1, d), bat),
            pl.BlockSpec((1, 1, d), bat),
            _resident(w_bf16.shape),
            _resident((1, ATT_HEAD_DIM)),
            _resident((1, ATT_HEAD_DIM)),
        ],
        out_specs=tuple(pl.BlockSpec((tm, width), row) for _ in range(4)),
        scratch_shapes=[pltpu.VMEM((tm, d), BF16)],
        compiler_params=_params(("arbitrary",)),
        name="att_proj",
    )(x2, norm_g, shift, scale, w_bf16, q_g, k_g)


def _attend(q, k, v_ext, bias, sg):
    s = lax.dot_general(q, k, (((1,), (1,)), ((), ())), preferred_element_type=F32) + bias
    m = jnp.max(s, axis=-1, keepdims=True)
    p = jnp.exp2(s - m).astype(BF16)
    o_ext = jnp.dot(p, v_ext, preferred_element_type=F32)
    o = o_ext[:, :ATT_HEAD_DIM] / o_ext[:, ATT_HEAD_DIM:]
    return (o * sg.astype(F32)).astype(BF16)


def _window_classes():
    chunks = ATT_QBLOCK // CHUNK
    far_chunks = LEFT_CHUNKS - MAX_REL // CHUNK
    table = []
    for ci in range(chunks):
        row = []
        for g in range(ATT_KWIN // 128):
            key_chunks = (2 * g, 2 * g + 1)
            valid = [ci <= ch <= ci + LEFT_CHUNKS for ch in key_chunks]
            far = [ci <= ch <= ci + far_chunks - 1 for ch in key_chunks]
            row.append("skip" if not any(valid) else "far" if all(far) else "near")
        table.append(row)
    return table


def _attend_full(q, k, v_ext, bias_ref, sg):
    s = lax.dot_general(q, k, (((1,), (1,)), ((), ())), preferred_element_type=F32)
    c_far = bias_ref[0:CHUNK, 0:128]
    p_rows = []
    for ci, classes in enumerate(_window_classes()):
        rows = slice(ci * CHUNK, (ci + 1) * CHUNK)
        logits = {}
        m_far = None
        m_near = None
        for g, cls in enumerate(classes):
            cols = slice(g * 128, (g + 1) * 128)
            if cls == "far":
                logits[g] = s[rows, cols]
                m_far = logits[g] if m_far is None else jnp.maximum(m_far, logits[g])
            elif cls == "near":
                logits[g] = s[rows, cols] + bias_ref[rows, cols]
                m_near = logits[g] if m_near is None else jnp.maximum(m_near, logits[g])
        m = jnp.max(jnp.maximum(m_far + c_far, m_near), axis=-1, keepdims=True)
        m = jnp.broadcast_to(m, (CHUNK, 128))
        m_minus_c = m - c_far
        parts = []
        for g, cls in enumerate(classes):
            if cls == "far":
                parts.append(jnp.exp2(logits[g] - m_minus_c).astype(BF16))
            elif cls == "near":
                parts.append(jnp.exp2(logits[g] - m).astype(BF16))
            else:
                parts.append(jnp.zeros((CHUNK, 128), BF16))
        p_rows.append(jnp.concatenate(parts, axis=1))
    p = jnp.concatenate(p_rows, axis=0)
    o_ext = jnp.dot(p, v_ext, preferred_element_type=F32)
    o = o_ext[:, :ATT_HEAD_DIM] / o_ext[:, ATT_HEAD_DIM:]
    return (o * sg.astype(F32)).astype(BF16)


def _attention_kernel(q_ref, k_ref, v_ref, sg_ref, rows_ref, o_ref, bias_ref):
    left = ATT_KWIN - ATT_QBLOCK
    n_blocks = q_ref.shape[0] // ATT_QBLOCK

    @pl.when(pl.program_id(1) == 0)
    def _():
        qi = lax.broadcasted_iota(jnp.int32, (ATT_QBLOCK, ATT_KWIN), 0)
        kj = lax.broadcasted_iota(jnp.int32, (ATT_QBLOCK, ATT_KWIN), 1)
        q_chunk = (qi + left) // CHUNK
        k_chunk = kj // CHUNK
        band = (k_chunk <= q_chunk) & (k_chunk >= q_chunk - LEFT_CHUNKS)
        for hh in range(2):
            tiled = jnp.broadcast_to(rows_ref[hh], (ATT_QBLOCK, ATT_BIAS_ROW))
            toeplitz = pltpu.roll(tiled, 0, 1, stride=1, stride_axis=0)
            bias_ref[hh] = jnp.where(band, toeplitz[:, :ATT_KWIN], NEG_INF)

    ones = jnp.ones((ATT_KWIN, ATT_HEAD_DIM), BF16)

    def run(row_start, key_start, n_keys):
        rows = pl.ds(row_start, ATT_QBLOCK)
        keys = pl.ds(key_start, n_keys)
        for hh in range(2):
            cols = slice(hh * ATT_HEAD_DIM, (hh + 1) * ATT_HEAD_DIM)
            v_ext = jnp.concatenate([v_ref[keys, cols], ones[:n_keys]], axis=1)
            if n_keys == ATT_KWIN:
                o_ref[rows, cols] = _attend_full(
                    q_ref[rows, cols], k_ref[keys, cols], v_ext, bias_ref.at[hh], sg_ref[rows, cols])
            else:
                o_ref[rows, cols] = _attend(
                    q_ref[rows, cols], k_ref[keys, cols], v_ext,
                    bias_ref[hh, :, ATT_KWIN - n_keys:], sg_ref[rows, cols])

    for blk in range(n_blocks):
        key_start = max(blk * ATT_QBLOCK - left, 0)
        run(blk * ATT_QBLOCK, key_start, (blk + 1) * ATT_QBLOCK - key_start)


def _bias_rows(rel_table):
    left = ATT_KWIN - ATT_QBLOCK
    t = rel_table.astype(F32) * LOG2E
    h = t.shape[0]
    far = jnp.broadcast_to(t[:, -1:], (h, left - MAX_REL))
    near = jnp.broadcast_to(t[:, :1], (h, ATT_KWIN - (left - MAX_REL) - t.shape[1]))
    wrap = jnp.broadcast_to(t[:, -1:], (h, ATT_BIAS_ROW - ATT_KWIN))
    return jnp.concatenate([far, t[:, ::-1], near, wrap], axis=1).reshape(h, 1, ATT_BIAS_ROW)


def _attention(q, k, v, sg, rel_table, batch, seq):
    n, width = q.shape
    assert seq % ATT_QBLOCK == 0
    pair = 2 * ATT_HEAD_DIM
    seq_pair = pl.BlockSpec((seq, pair), lambda h, b: (b, h))
    return pl.pallas_call(
        _attention_kernel,
        out_shape=jax.ShapeDtypeStruct((n, width), BF16),
        grid=(ATT_HEADS // 2, batch),
        in_specs=[seq_pair, seq_pair, seq_pair, seq_pair,
                  pl.BlockSpec((2, 1, ATT_BIAS_ROW), lambda h, b: (h, 0, 0))],
        out_specs=seq_pair,
        scratch_shapes=[pltpu.VMEM((2, ATT_QBLOCK, ATT_KWIN), F32)],
        compiler_params=_params(("arbitrary", "arbitrary")),
        name="attention",
    )(q, k, v, sg, _bias_rows(rel_table))


def kernel(x, c, positions, norm_g, ada_w, ada_b, ret_w_in, ret_gn_g, ret_w_out,
           att_w_in, att_q_g, att_k_g, att_rel_bias, att_w_out):
    batch, seq, d = x.shape
    n = batch * seq
    mod = _adaln(c, ada_w, ada_b)
    shift = mod[:, :, :d].reshape(-1, batch, 1, d)
    scale = mod[:, :, d:2 * d].reshape(-1, batch, 1, d)
    gate = mod[:, :, 2 * d:].reshape(-1, batch, 1, d)

    x2 = x.reshape(n, d)
    pos2 = positions.reshape(n, 1)
    freq = (1.0 / (ROPE_BASE ** (jnp.arange(0, RET_QK_DIM, 2, dtype=F32) / RET_QK_DIM))).reshape(1, -1)

    proj0 = functools.partial(_ret_proj, x2, pos2, freq, norm_g[0].reshape(1, d), shift[0], scale[0],
                              ret_w_in[0].astype(BF16), seq)
    small = jnp.all((positions > -SMALL_ANGLE_LIMIT) & (positions < SMALL_ANGLE_LIMIT))
    q, k, v, sg = lax.cond(small, lambda: proj0(True), lambda: proj0(False))
    x2 = _retention(q, k, v, sg, ret_gn_g[0], ret_w_out[0].astype(BF16), x2, gate[0], batch, seq)

    q, k, v, sg = _att_proj(x2, norm_g[1].reshape(1, d), shift[1], scale[1],
                            att_w_in[0].astype(BF16), att_q_g[0].reshape(1, -1),
                            att_k_g[0].reshape(1, -1), seq)
    o = _attention(q, k, v, sg, att_rel_bias[0], batch, seq)
    x2 = _out_proj(o, att_w_out[0].astype(BF16), x2, gate[1], seq)
    return x2.reshape(batch, seq, d)
```

```python
import functools
import math

import jax
import jax.numpy as jnp
from jax import lax
from jax.experimental import pallas as pl
from jax.experimental.pallas import tpu as pltpu

EPS = 1e-6
CHUNK = 64

RET_HEADS = 4
RET_QK_DIM = 256
RET_V_DIM = 512
ROPE_BASE = 10000.0

ATT_HEADS = 16
ATT_HEAD_DIM = 128
LEFT_CHUNKS = 8
MAX_REL = 2 * CHUNK
NEG_INF = -1e30

PROJ_ROWS = 512
RET_PROJ_ROWS = 1024
OUT_PROJ_ROWS = 1024
RET_BLOCK = 256
RET_STEP_ROWS = 1024
ATT_QBLOCK = 256
ATT_KWIN = ATT_QBLOCK + LEFT_CHUNKS * CHUNK
ATT_BIAS_ROW = 1024
LOG2E = math.log2(math.e)

VMEM_LIMIT_BYTES = 56 * 1024 * 1024

BF16 = jnp.bfloat16
F32 = jnp.float32


def _resident(shape):
    return pl.BlockSpec(shape, lambda *_: (0,) * len(shape), pipeline_mode=pl.Buffered(1))


def _params(semantics):
    return pltpu.CompilerParams(dimension_semantics=semantics, vmem_limit_bytes=VMEM_LIMIT_BYTES)


def _adaln_kernel(c_ref, w_ref, b_ref, o_ref):
    c = c_ref[...]
    cond = c * (1.0 / (1.0 + jnp.exp(-c)))
    o_ref[0] = jnp.dot(cond, w_ref[0], preferred_element_type=F32) + b_ref[0]


def _adaln(c, ada_w, ada_b):
    depth, d, d3 = ada_w.shape
    b = c.shape[0]
    col = 1024
    return pl.pallas_call(
        _adaln_kernel,
        out_shape=jax.ShapeDtypeStruct((depth, b, d3), F32),
        grid=(depth, d3 // col),
        in_specs=[
            pl.BlockSpec((b, d), lambda i, j: (0, 0)),
            pl.BlockSpec((1, d, col), lambda i, j: (i, 0, j)),
            pl.BlockSpec((1, 1, col), lambda i, j: (i, 0, j)),
        ],
        out_specs=pl.BlockSpec((1, b, col), lambda i, j: (i, 0, j)),
        compiler_params=_params(("arbitrary", "arbitrary")),
        name="adaln",
    )(c, ada_w, ada_b.reshape(depth, 1, d3))


def _modulated_norm(x, g_ref, shift_ref, scale_ref):
    ms = jnp.mean(x * x, axis=-1, keepdims=True)
    h = x * lax.rsqrt(ms + EPS) * g_ref[...]
    h = h * (1.0 + scale_ref[0]) + shift_ref[0]
    return h.astype(BF16)


def _silu(t):
    half = 0.5 * t
    return half + half * jnp.tanh(half)


HALF_PI_PARTS = (1.578125, -0.00732421875, -4.470348358154297e-06, 1.5893254712295857e-08)
SMALL_ANGLE_LIMIT = 2 ** 17
SIN_TAYLOR = tuple((-1.0) ** i / math.factorial(2 * i + 1) for i in range(8))
COS_TAYLOR = tuple((-1.0) ** i / math.factorial(2 * i) for i in range(8))


def _sincos_small(x):
    k = jnp.floor(x * (1.0 / math.pi) + 0.5)
    r = x
    for part in HALF_PI_PARTS:
        r = r - k * (2.0 * part)
    z = r * r
    s = SIN_TAYLOR[-1]
    c = COS_TAYLOR[-1]
    for s_coef, c_coef in zip(SIN_TAYLOR[-2::-1], COS_TAYLOR[-2::-1]):
        s = s * z + s_coef
        c = c * z + c_coef
    sign = 1.0 - 2.0 * (k - 2.0 * jnp.floor(0.5 * k))
    return sign * c, sign * (s * r)


def _ret_proj_kernel(x_ref, pos_ref, freq_ref, g_ref, shift_ref, scale_ref, w_ref,
                     q_ref, k_ref, v_ref, sg_ref, h_ref, cos_ref, sin_ref, *, small_angles):
    h_ref[...] = _modulated_norm(x_ref[...], g_ref, shift_ref, scale_ref)
    ang = pos_ref[...].astype(F32) * freq_ref[...]
    if small_angles:
        cos_ref[...], sin_ref[...] = _sincos_small(ang)
    else:
        cos_ref[...] = jnp.cos(ang)
        sin_ref[...] = jnp.sin(ang)
    cos = cos_ref[...]
    sin = sin_ref[...]
    half = RET_QK_DIM // 2
    qk_width = RET_HEADS * RET_QK_DIM
    v_width = RET_HEADS * RET_V_DIM

    def proj(col, width):
        return jnp.dot(h_ref[...], w_ref[:, col:col + width], preferred_element_type=F32)

    for out_ref, base in ((q_ref, 0), (k_ref, qk_width)):
        for hh in range(RET_HEADS):
            acc = proj(base + hh * RET_QK_DIM, RET_QK_DIM)
            t1 = acc[:, :half]
            t2 = acc[:, half:]
            out_ref[:, hh * RET_QK_DIM:hh * RET_QK_DIM + half] = (t1 * cos - t2 * sin).astype(BF16)
            out_ref[:, hh * RET_QK_DIM + half:(hh + 1) * RET_QK_DIM] = (t1 * sin + t2 * cos).astype(BF16)
    for c in range(0, v_width, RET_V_DIM):
        sg_ref[:, c:c + RET_V_DIM] = _silu(proj(2 * qk_width + v_width + c, RET_V_DIM)).astype(BF16)
    for c in range(0, v_width, RET_V_DIM):
        v_ref[:, c:c + RET_V_DIM] = proj(2 * qk_width + c, RET_V_DIM).astype(BF16)


def _ret_proj(x2, pos2, freq, norm_g, shift, scale, w_bf16, seq, small_angles):
    n, d = x2.shape
    tm = RET_PROJ_ROWS
    assert n % tm == 0 and seq % tm == 0
    per_seq = seq // tm
    qk_width = RET_HEADS * RET_QK_DIM
    v_width = RET_HEADS * RET_V_DIM
    half = RET_QK_DIM // 2
    row = lambda i: (i, 0)
    bat = lambda i: (i // per_seq, 0, 0)
    return pl.pallas_call(
        functools.partial(_ret_proj_kernel, small_angles=small_angles),
        out_shape=(
            jax.ShapeDtypeStruct((n, qk_width), BF16),
            jax.ShapeDtypeStruct((n, qk_width), BF16),
            jax.ShapeDtypeStruct((n, v_width), BF16),
            jax.ShapeDtypeStruct((n, v_width), BF16),
        ),
        grid=(n // tm,),
        in_specs=[
            pl.BlockSpec((tm, d), row),
            pl.BlockSpec((tm, 1), row),
            _resident((1, half)),
            _resident((1, d)),
            pl.BlockSpec((1, 1, d), bat),
            pl.BlockSpec((1, 1, d), bat),
            _resident(w_bf16.shape),
        ],
        out_specs=(
            pl.BlockSpec((tm, qk_width), row),
            pl.BlockSpec((tm, qk_width), row),
            pl.BlockSpec((tm, v_width), row),
            pl.BlockSpec((tm, v_width), row),
        ),
        scratch_shapes=[pltpu.VMEM((tm, d), BF16), pltpu.VMEM((tm, half), F32), pltpu.VMEM((tm, half), F32)],
        compiler_params=_params(("arbitrary",)),
        name="ret_proj",
    )(x2, pos2, freq, norm_g, shift, scale, w_bf16)


def _retention_kernel(q_ref, k_ref, v_ref, sg_ref, dmask_ref, qd_ref, kd_ref, bd_ref, gn_ref,
                      w_ref, x_ref, gate_ref, y_ref, state_ref):
    @pl.when(pl.program_id(1) == 0)
    def _():
        state_ref[...] = jnp.zeros_like(state_ref)

    for blk in range(RET_STEP_ROWS // RET_BLOCK):
        rows = slice(blk * RET_BLOCK, (blk + 1) * RET_BLOCK)
        gated = []
        for head in range(RET_HEADS):
            qk_cols = slice(head * RET_QK_DIM, (head + 1) * RET_QK_DIM)
            v_cols = slice(head * RET_V_DIM, (head + 1) * RET_V_DIM)
            q = q_ref[rows, qk_cols]
            k = k_ref[rows, qk_cols]
            v = v_ref[rows, v_cols]
            state = state_ref[head]
            scores = lax.dot_general(q, k, (((1,), (1,)), ((), ())), preferred_element_type=F32)
            o = jnp.dot((scores * dmask_ref[head]).astype(BF16), v, preferred_element_type=F32)
            qs = q * qd_ref[head]
            o = o + jnp.dot(qs, state.astype(BF16), preferred_element_type=F32)
            ks = k * kd_ref[head]
            state_ref[head] = bd_ref[head] * state + lax.dot_general(
                ks, v, (((0,), (0,)), ((), ())), preferred_element_type=F32)
            ms = jnp.mean(o * o, axis=-1, keepdims=True)
            o = o * lax.rsqrt(ms + EPS) * gn_ref[head]
            gated.append(o.astype(BF16) * sg_ref[rows, v_cols])
        out = jnp.dot(jnp.concatenate(gated, axis=1), w_ref[...], preferred_element_type=F32)
        y_ref[rows, :] = x_ref[rows, :] + gate_ref[0] * out


def _retention_tables():
    log_gamma = jnp.log(1.0 - 2.0 ** (-5.0 - jnp.arange(RET_HEADS, dtype=F32)))
    t = jnp.arange(RET_BLOCK, dtype=F32)
    dist = t[:, None] - t[None, :]
    chunk = jnp.arange(RET_BLOCK) // CHUNK
    visible = chunk[None, :] <= chunk[:, None]
    k_scale = RET_QK_DIM ** -0.5
    dmask = jnp.where(visible[None], jnp.exp(log_gamma[:, None, None] * jnp.abs(dist)[None]), 0.0) * k_scale
    qd = jnp.exp(log_gamma[:, None] * (t + 1.0))
    kd = jnp.exp(log_gamma[:, None] * (RET_BLOCK - 1.0 - t)) * k_scale
    qd = jnp.broadcast_to(qd[:, :, None], (RET_HEADS, RET_BLOCK, RET_QK_DIM))
    kd = jnp.broadcast_to(kd[:, :, None], (RET_HEADS, RET_BLOCK, RET_QK_DIM))
    bd = jnp.broadcast_to(jnp.exp(log_gamma * RET_BLOCK)[:, None, None], (RET_HEADS, 1, RET_V_DIM))
    return dmask.astype(F32), qd.astype(BF16), kd.astype(BF16), bd.astype(F32)


def _retention(q, k, v, sg, gn_g, w_out_bf16, x2, gate, batch, seq):
    n, d = x2.shape
    rows = RET_STEP_ROWS
    assert seq % rows == 0
    per_seq = seq // rows
    dmask, qd, kd, bd = _retention_tables()
    qk_width = RET_HEADS * RET_QK_DIM
    v_width = RET_HEADS * RET_V_DIM
    row = lambda b, j: (b * per_seq + j, 0)
    return pl.pallas_call(
        _retention_kernel,
        out_shape=jax.ShapeDtypeStruct((n, d), F32),
        grid=(batch, per_seq),
        in_specs=[
            pl.BlockSpec((rows, qk_width), row),
            pl.BlockSpec((rows, qk_width), row),
            pl.BlockSpec((rows, v_width), row),
            pl.BlockSpec((rows, v_width), row),
            _resident(dmask.shape),
            _resident(qd.shape),
            _resident(kd.shape),
            _resident(bd.shape),
            _resident((RET_HEADS, 1, RET_V_DIM)),
            _resident(w_out_bf16.shape),
            pl.BlockSpec((rows, d), row),
            pl.BlockSpec((1, 1, d), lambda b, j: (b, 0, 0)),
        ],
        out_specs=pl.BlockSpec((rows, d), row),
        scratch_shapes=[pltpu.VMEM((RET_HEADS, RET_QK_DIM, RET_V_DIM), F32)],
        compiler_params=_params(("arbitrary", "arbitrary")),
        name="retention",
    )(q, k, v, sg, dmask, qd, kd, bd, gn_g.reshape(RET_HEADS, 1, RET_V_DIM), w_out_bf16, x2, gate)


def _out_proj_kernel(o_ref, w_ref, x_ref, gate_ref, y_ref):
    out = jnp.dot(o_ref[...], w_ref[...], preferred_element_type=F32)
    y_ref[...] = x_ref[...] + gate_ref[0] * out


def _out_proj(o, w_bf16, x2, gate, seq):
    n, d = x2.shape
    width = o.shape[1]
    tm = OUT_PROJ_ROWS
    assert n % tm == 0 and seq % tm == 0
    per_seq = seq // tm
    row = lambda i: (i, 0)
    return pl.pallas_call(
        _out_proj_kernel,
        out_shape=jax.ShapeDtypeStruct((n, d), F32),
        grid=(n // tm,),
        in_specs=[
            pl.BlockSpec((tm, width), row),
            _resident(w_bf16.shape),
            pl.BlockSpec((tm, d), row),
            pl.BlockSpec((1, 1, d), lambda i: (i // per_seq, 0, 0)),
        ],
        out_specs=pl.BlockSpec((tm, d), row),
        compiler_params=_params(("arbitrary",)),
        name="out_proj",
    )(o, w_bf16, x2, gate)


def _att_proj_kernel(x_ref, g_ref, shift_ref, scale_ref, w_ref, qg_ref, kg_ref,
                     q_ref, k_ref, v_ref, sg_ref, h_ref):
    h_ref[...] = _modulated_norm(x_ref[...], g_ref, shift_ref, scale_ref)
    width = ATT_HEADS * ATT_HEAD_DIM
    chunk = 512

    def proj(col, w):
        return jnp.dot(h_ref[...], w_ref[:, col:col + w], preferred_element_type=F32)

    for out_ref, gain_ref, base, post in ((q_ref, qg_ref, 0, ATT_HEAD_DIM ** -0.5 * LOG2E), (k_ref, kg_ref, width, 1.0)):
        gain = gain_ref[...] * post
        for c in range(0, width, chunk):
            acc = proj(base + c, chunk)
            for hh in range(chunk // ATT_HEAD_DIM):
                a = acc[:, hh * ATT_HEAD_DIM:(hh + 1) * ATT_HEAD_DIM]
                ms = jnp.mean(a * a, axis=-1, keepdims=True)
                out_ref[:, c + hh * ATT_HEAD_DIM:c + (hh + 1) * ATT_HEAD_DIM] = (
                    a * lax.rsqrt(ms + EPS) * gain).astype(BF16)
    for c in range(0, width, chunk):
        sg_ref[:, c:c + chunk] = _silu(proj(3 * width + c, chunk)).astype(BF16)
    for c in range(0, width, chunk):
        v_ref[:, c:c + chunk] = proj(2 * width + c, chunk).astype(BF16)


def _att_proj(x2, norm_g, shift, scale, w_bf16, q_g, k_g, seq):
    n, d = x2.shape
    tm = PROJ_ROWS
    assert n % tm == 0 and seq % tm == 0
    per_seq = seq // tm
    width = ATT_HEADS * ATT_HEAD_DIM
    row = lambda i: (i, 0)
    bat = lambda i: (i // per_seq, 0, 0)
    out = jax.ShapeDtypeStruct((n, width), BF16)
    return pl.pallas_call(
        _att_proj_kernel,
        out_shape=(out, out, out, out),
        grid=(n // tm,),
        in_specs=[
            pl.BlockSpec((tm, d), row),
            _resident((1, d)),
            pl.BlockSpec((1, 1, d), bat),
            pl.BlockSpec((1, 1, d), bat),
            _resident(w_bf16.shape),
            _resident((1, ATT_HEAD_DIM)),
            _resident((1, ATT_HEAD_DIM)),
        ],
        out_specs=tuple(pl.BlockSpec((tm, width), row) for _ in range(4)),
        scratch_shapes=[pltpu.VMEM((tm, d), BF16)],
        compiler_params=_params(("arbitrary",)),
        name="att_proj",
    )(x2, norm_g, shift, scale, w_bf16, q_g, k_g)


def _attend(q, k, v_ext, bias, sg):
    s = lax.dot_general(q, k, (((1,), (1,)), ((), ())), preferred_element_type=F32) + bias
    m = jnp.max(s, axis=-1, keepdims=True)
    p = jnp.exp2(s - m).astype(BF16)
    o_ext = jnp.dot(p, v_ext, preferred_element_type=F32)
    o = o_ext[:, :ATT_HEAD_DIM] / o_ext[:, ATT_HEAD_DIM:]
    return (o * sg.astype(F32)).astype(BF16)


def _window_classes():
    chunks = ATT_QBLOCK // CHUNK
    far_chunks = LEFT_CHUNKS - MAX_REL // CHUNK
    table = []
    for ci in range(chunks):
        row = []
        for g in range(ATT_KWIN // 128):
            key_chunks = (2 * g, 2 * g + 1)
            valid = [ci <= ch <= ci + LEFT_CHUNKS for ch in key_chunks]
            far = [ci <= ch <= ci + far_chunks - 1 for ch in key_chunks]
            row.append("skip" if not any(valid) else "far" if all(far) else "near")
        table.append(row)
    return table


def _attend_full(q, k, v_ext, bias_ref, sg):
    s = lax.dot_general(q, k, (((1,), (1,)), ((), ())), preferred_element_type=F32)
    c_far = bias_ref[0:CHUNK, 0:128]
    p_rows = []
    for ci, classes in enumerate(_window_classes()):
        rows = slice(ci * CHUNK, (ci + 1) * CHUNK)
        logits = {}
        m_far = None
        m_near = None
        for g, cls in enumerate(classes):
            cols = slice(g * 128, (g + 1) * 128)
            if cls == "far":
                logits[g] = s[rows, cols]
                m_far = logits[g] if m_far is None else jnp.maximum(m_far, logits[g])
            elif cls == "near":
                logits[g] = s[rows, cols] + bias_ref[rows, cols]
                m_near = logits[g] if m_near is None else jnp.maximum(m_near, logits[g])
        m = jnp.max(jnp.maximum(m_far + c_far, m_near), axis=-1, keepdims=True)
        m = jnp.broadcast_to(m, (CHUNK, 128))
        m_minus_c = m - c_far
        parts = []
        for g, cls in enumerate(classes):
            if cls == "far":
                parts.append(jnp.exp2(logits[g] - m_minus_c).astype(BF16))
            elif cls == "near":
                parts.append(jnp.exp2(logits[g] - m).astype(BF16))
            else:
                parts.append(jnp.zeros((CHUNK, 128), BF16))
        p_rows.append(jnp.concatenate(parts, axis=1))
    p = jnp.concatenate(p_rows, axis=0)
    o_ext = jnp.dot(p, v_ext, preferred_element_type=F32)
    o = o_ext[:, :ATT_HEAD_DIM] / o_ext[:, ATT_HEAD_DIM:]
    return (o * sg.astype(F32)).astype(BF16)


def _attention_kernel(q_ref, k_ref, v_ref, sg_ref, rows_ref, o_ref, bias_ref):
    left = ATT_KWIN - ATT_QBLOCK
    n_blocks = q_ref.shape[0] // ATT_QBLOCK

    @pl.when(pl.program_id(1) == 0)
    def _():
        qi = lax.broadcasted_iota(jnp.int32, (ATT_QBLOCK, ATT_KWIN), 0)
        kj = lax.broadcasted_iota(jnp.int32, (ATT_QBLOCK, ATT_KWIN), 1)
        q_chunk = (qi + left) // CHUNK
        k_chunk = kj // CHUNK
        band = (k_chunk <= q_chunk) & (k_chunk >= q_chunk - LEFT_CHUNKS)
        for hh in range(2):
            tiled = jnp.broadcast_to(rows_ref[hh], (ATT_QBLOCK, ATT_BIAS_ROW))
            toeplitz = pltpu.roll(tiled, 0, 1, stride=1, stride_axis=0)
            bias_ref[hh] = jnp.where(band, toeplitz[:, :ATT_KWIN], NEG_INF)

    ones = jnp.ones((ATT_KWIN, ATT_HEAD_DIM), BF16)

    def run(row_start, key_start, n_keys):
        rows = pl.ds(row_start, ATT_QBLOCK)
        keys = pl.ds(key_start, n_keys)
        for hh in range(2):
            cols = slice(hh * ATT_HEAD_DIM, (hh + 1) * ATT_HEAD_DIM)
            v_ext = jnp.concatenate([v_ref[keys, cols], ones[:n_keys]], axis=1)
            if n_keys == ATT_KWIN:
                o_ref[rows, cols] = _attend_full(
                    q_ref[rows, cols], k_ref[keys, cols], v_ext, bias_ref.at[hh], sg_ref[rows, cols])
            else:
                o_ref[rows, cols] = _attend(
                    q_ref[rows, cols], k_ref[keys, cols], v_ext,
                    bias_ref[hh, :, ATT_KWIN - n_keys:], sg_ref[rows, cols])

    for blk in range(n_blocks):
        key_start = max(blk * ATT_QBLOCK - left, 0)
        run(blk * ATT_QBLOCK, key_start, (blk + 1) * ATT_QBLOCK - key_start)


def _bias_rows(rel_table):
    left = ATT_KWIN - ATT_QBLOCK
    t = rel_table.astype(F32) * LOG2E
    h = t.shape[0]
    far = jnp.broadcast_to(t[:, -1:], (h, left - MAX_REL))
    near = jnp.broadcast_to(t[:, :1], (h, ATT_KWIN - (left - MAX_REL) - t.shape[1]))
    wrap = jnp.broadcast_to(t[:, -1:], (h, ATT_BIAS_ROW - ATT_KWIN))
    return jnp.concatenate([far, t[:, ::-1], near, wrap], axis=1).reshape(h, 1, ATT_BIAS_ROW)


def _attention(q, k, v, sg, rel_table, batch, seq):
    n, width = q.shape
    assert seq % ATT_QBLOCK == 0
    pair = 2 * ATT_HEAD_DIM
    seq_pair = pl.BlockSpec((seq, pair), lambda h, b: (b, h))
    return pl.pallas_call(
        _attention_kernel,
        out_shape=jax.ShapeDtypeStruct((n, width), BF16),
        grid=(ATT_HEADS // 2, batch),
        in_specs=[seq_pair, seq_pair, seq_pair, seq_pair,
                  pl.BlockSpec((2, 1, ATT_BIAS_ROW), lambda h, b: (h, 0, 0))],
        out_specs=seq_pair,
        scratch_shapes=[pltpu.VMEM((2, ATT_QBLOCK, ATT_KWIN), F32)],
        compiler_params=_params(("arbitrary", "arbitrary")),
        name="attention",
    )(q, k, v, sg, _bias_rows(rel_table))


def kernel(x, c, positions, norm_g, ada_w, ada_b, ret_w_in, ret_gn_g, ret_w_out,
           att_w_in, att_q_g, att_k_g, att_rel_bias, att_w_out):
    batch, seq, d = x.shape
    n = batch * seq
    mod = _adaln(c, ada_w, ada_b)
    shift = mod[:, :, :d].reshape(-1, batch, 1, d)
    scale = mod[:, :, d:2 * d].reshape(-1, batch, 1, d)
    gate = mod[:, :, 2 * d:].reshape(-1, batch, 1, d)

    x2 = x.reshape(n, d)
    pos2 = positions.reshape(n, 1)
    freq = (1.0 / (ROPE_BASE ** (jnp.arange(0, RET_QK_DIM, 2, dtype=F32) / RET_QK_DIM))).reshape(1, -1)

    proj0 = functools.partial(_ret_proj, x2, pos2, freq, norm_g[0].reshape(1, d), shift[0], scale[0],
                              ret_w_in[0].astype(BF16), seq)
    small = jnp.all((positions > -SMALL_ANGLE_LIMIT) & (positions < SMALL_ANGLE_LIMIT))
    q, k, v, sg = lax.cond(small, lambda: proj0(True), lambda: proj0(False))
    x2 = _retention(q, k, v, sg, ret_gn_g[0], ret_w_out[0].astype(BF16), x2, gate[0], batch, seq)

    q, k, v, sg = _att_proj(x2, norm_g[1].reshape(1, d), shift[1], scale[1],
                            att_w_in[0].astype(BF16), att_q_g[0].reshape(1, -1),
                            att_k_g[0].reshape(1, -1), seq)
    o = _attention(q, k, v, sg, att_rel_bias[0], batch, seq)
    x2 = _out_proj(o, att_w_out[0].astype(BF16), x2, gate[1], seq)
    return x2.reshape(batch, seq, d)
```
